```python
import math
import jax, jax.numpy as jnp
from jax import lax
import numpy as np


D_MODEL = 1024
BATCH = 16
SEQ = 2048
DEPTH = 2

CTX_LEN = 256
GRID_W = 64
EPS = 1e-6
ROPE_BASE = 10000.0
RET_HEADS = 4
RET_DK = 64
RET_DV = 64
RET_CHUNK = 128
RET_QK = RET_HEADS * RET_DK
RET_WIDTH = RET_HEADS * RET_DV
DIFF_HEADS = 4
DIFF_DK = 64
DIFF_DV = 2 * DIFF_DK
DIFF_QK = DIFF_HEADS * 2 * DIFF_DK
DIFF_WIDTH = DIFF_HEADS * DIFF_DV
Q_BLOCK = 128
CONV_CH = D_MODEL // 4
CONV_WIDTH = 31
CONV_PAD = (CONV_WIDTH - 1) // 2
MIX_WIDTH = RET_WIDTH + DIFF_WIDTH + CONV_CH
IN_WIDTH = 2 * RET_QK + 2 * RET_WIDTH + 2 * DIFF_QK + DIFF_WIDTH + 2 * CONV_CH
D_FF = 4 * D_MODEL
N_MOD = 6

kernel_name = 'hybrid_retention_diffattn_conformer_dit'


def rmsnorm(x, w):
    xf = x.astype(jnp.float32)
    y = xf * lax.rsqrt(jnp.mean(xf * xf, axis=-1, keepdims=True) + EPS)
    return (y * w).astype(x.dtype)


def layernorm(x, w, b):
    xf = x.astype(jnp.float32)
    mu = jnp.mean(xf, axis=-1, keepdims=True)
    var = jnp.mean(jnp.square(xf - mu), axis=-1, keepdims=True)
    return (xf - mu) * lax.rsqrt(var + EPS) * w + b


def modulate(h, shift, scale):
    return h * (1.0 + scale) + shift


def split_proj(p):
    sizes = (RET_QK, RET_QK, RET_WIDTH, RET_WIDTH, DIFF_QK, DIFF_QK, DIFF_WIDTH, CONV_CH, CONV_CH)
    out, start = [], 0
    for s in sizes:
        out.append(p[..., start:start + s])
        start += s
    return out


def rope_angles(pos, dim):
    inv = ROPE_BASE ** (-jnp.arange(0, dim, 2, dtype=jnp.float32) / dim)
    ang = pos.astype(jnp.float32)[:, None] * inv[None, :]
    return jnp.cos(ang), jnp.sin(ang)


def apply_rope(x, cos, sin):
    half = x.shape[-1] // 2
    shape = (cos.shape[0],) + (1,) * (x.ndim - 3) + (half,)
    cos = cos.reshape(shape)
    sin = sin.reshape(shape)
    x1, x2 = x[..., :half], x[..., half:]
    return jnp.concatenate([x1 * cos - x2 * sin, x2 * cos + x1 * sin], axis=-1).astype(x.dtype)


def axial_rope(x, rows, cols):
    h = x.shape[-1] // 2
    cr, sr = rope_angles(rows, h)
    cc, sc = rope_angles(cols, h)
    return jnp.concatenate([apply_rope(x[..., :h], cr, sr), apply_rope(x[..., h:], cc, sc)], axis=-1)


def retention_chunkwise(q, k, v, log_gamma, s0):
    b, h, l, dk = q.shape
    dv = v.shape[-1]
    n = l // RET_CHUNK
    qc = q.reshape(b, h, n, RET_CHUNK, dk)
    kc = k.reshape(b, h, n, RET_CHUNK, dk)
    vc = v.reshape(b, h, n, RET_CHUNK, dv)
    idx = jnp.arange(RET_CHUNK, dtype=jnp.float32)
    lg = log_gamma[:, None]
    rel = idx[:, None] - idx[None, :]
    intra = jnp.where(rel >= 0, jnp.exp(lg[:, :, None] * jnp.maximum(rel, 0.0)), 0.0)
    scores = jnp.einsum('bhncd,bhnsd->bhncs', qc, kc) * intra[:, None]
    o_intra = jnp.einsum('bhncs,bhnse->bhnce', scores, vc)
    q_decay = jnp.exp(lg * (idx + 1.0))
    k_decay = jnp.exp(lg * (RET_CHUNK - 1.0 - idx))
    chunk_decay = jnp.exp(log_gamma * RET_CHUNK)[None, :, None, None]
    kv = jnp.einsum('bhncd,hc,bhnce->nbhde', kc, k_decay, vc)

    def step(s, kv_n):
        return chunk_decay * s + kv_n, s

    _, s_prev = lax.scan(step, s0, kv)
    o_cross = jnp.einsum('bhncd,hc,nbhde->bhnce', qc, q_decay, s_prev)
    return (o_intra + o_cross).reshape(b, h, l, dv)


def retention_final_state(k, v, log_gamma):
    l = k.shape[2]
    w = jnp.exp(log_gamma[:, None] * (l - 1.0 - jnp.arange(l, dtype=jnp.float32)))
    return jnp.einsum('bhld,hl,bhle->bhde', k, w, v)


def retention_output(o, g, gn_w):
    b, h, l, dv = o.shape
    t = layernorm(o.transpose(0, 2, 1, 3), gn_w.reshape(h, dv), 0.0)
    return t.reshape(b, l, h * dv) * jax.nn.silu(g)


def diff_attend(q, k, v, lam):
    b, lq, h, _, dk = q.shape
    nb = lq // Q_BLOCK
    qb = jnp.moveaxis(q.reshape(b, nb, Q_BLOCK, h, 2, dk), 1, 0)
    scale = dk ** -0.5

    def block(qblk):
        s = jnp.einsum('bqhmd,bkhmd->bhmqk', qblk, k).astype(jnp.float32) * scale
        p = jax.nn.softmax(s, axis=-1)
        a = p[:, :, 0] - lam * p[:, :, 1]
        return jnp.einsum('bhqk,bkhe->bqhe', a.astype(v.dtype), v)

    o = lax.map(block, qb)
    return jnp.moveaxis(o, 0, 1).reshape(b, lq, h, v.shape[-1])


def diff_output(o, gn_w, lam_init):
    b, l, h, dv = o.shape
    return (rmsnorm(o, gn_w.reshape(h, dv)) * (1.0 - lam_init)).reshape(b, l, h * dv)


def conformer_conv(val, gate, conv_w, conv_b, ln_w, ln_b):
    u = val * jax.nn.sigmoid(gate)
    y = lax.conv_general_dilated(u, conv_w[:, None, :].astype(u.dtype), window_strides=(1,),
                                 padding=[(CONV_PAD, CONV_PAD)],
                                 dimension_numbers=('NWC', 'WIO', 'NWC'),
                                 feature_group_count=CONV_CH) + conv_b
    return jax.nn.silu(layernorm(y, ln_w, ln_b)).astype(val.dtype)


def sq_relu_mlp(h, w1, w2):
    return jnp.square(jax.nn.relu(h @ w1)) @ w2


def hybrid_mixer(hx, hc, w_in, ret_log_decay, ret_gn_w, diff_lam, lam_init, diff_gn_w,
                 conv_w, conv_b, conv_ln_w, conv_ln_b, w_out, rows, cols, need_ctx_out):
    b, l, _ = hx.shape
    lc = hc.shape[1]
    rq_x, rk_x, rv_x, rg_x, dq_x, dk_x, dv_x, cv_x, cg_x = split_proj(hx @ w_in)
    rq_c, rk_c, rv_c, rg_c, dq_c, dk_c, dv_c, cv_c, cg_c = split_proj(hc @ w_in)
    flip = lambda t: jnp.flip(t, axis=2)

    cos1, sin1 = rope_angles(jnp.arange(l), RET_DK)
    rk_scale = RET_DK ** -0.5
    q_lat = apply_rope(rq_x.reshape(b, l, RET_HEADS, RET_DK), cos1, sin1).transpose(0, 2, 1, 3)
    k_lat = (apply_rope(rk_x.reshape(b, l, RET_HEADS, RET_DK), cos1, sin1) * rk_scale).transpose(0, 2, 1, 3)
    v_lat = rv_x.reshape(b, l, RET_HEADS, RET_DV).transpose(0, 2, 1, 3)
    k_ctx = (rk_c.reshape(b, lc, RET_HEADS, RET_DK) * rk_scale).transpose(0, 2, 1, 3)
    v_ctx = rv_c.reshape(b, lc, RET_HEADS, RET_DV).transpose(0, 2, 1, 3)
    log_g = -jnp.exp(ret_log_decay.astype(jnp.float32))
    s_fwd = retention_final_state(k_ctx, v_ctx, log_g[0])
    s_bwd = retention_final_state(flip(k_ctx), flip(v_ctx), log_g[1])
    o_f = retention_chunkwise(q_lat, k_lat, v_lat, log_g[0], s_fwd)
    o_b = retention_chunkwise(flip(q_lat), flip(k_lat), flip(v_lat), log_g[1], s_bwd)
    ret_x = retention_output(o_f + flip(o_b), rg_x, ret_gn_w)

    dq_lat = axial_rope(dq_x.reshape(b, l, DIFF_HEADS, 2, DIFF_DK), rows, cols)
    dk_lat = axial_rope(dk_x.reshape(b, l, DIFF_HEADS, 2, DIFF_DK), rows, cols)
    dv_lat = dv_x.reshape(b, l, DIFF_HEADS, DIFF_DV)
    dk_ctx = dk_c.reshape(b, lc, DIFF_HEADS, 2, DIFF_DK)
    dv_ctx = dv_c.reshape(b, lc, DIFF_HEADS, DIFF_DV)
    lam = (jnp.exp(jnp.sum(diff_lam[0] * diff_lam[1])) - jnp.exp(jnp.sum(diff_lam[2] * diff_lam[3]))
           + lam_init).astype(jnp.float32)
    k_all = jnp.concatenate([dk_lat, dk_ctx], axis=1)
    v_all = jnp.concatenate([dv_lat, dv_ctx], axis=1)
    diff_x = diff_output(diff_attend(dq_lat, k_all, v_all, lam), diff_gn_w, lam_init)

    conv_x = conformer_conv(cv_x, cg_x, conv_w, conv_b, conv_ln_w, conv_ln_b)

    y_x = (jnp.concatenate([ret_x, diff_x, conv_x], axis=-1) @ w_out).astype(hx.dtype)
    if not need_ctx_out:
        return y_x, None

    q_ctx = rq_c.reshape(b, lc, RET_HEADS, RET_DK).transpose(0, 2, 1, 3)
    s0 = jnp.zeros((b, RET_HEADS, RET_DK, RET_DV), jnp.float32)
    oc_f = retention_chunkwise(q_ctx, k_ctx, v_ctx, log_g[0], s0)
    oc_b = retention_chunkwise(flip(q_ctx), flip(k_ctx), flip(v_ctx), log_g[1], s0)
    ret_c = retention_output(oc_f + flip(oc_b), rg_c, ret_gn_w)
    dq_ctx = dq_c.reshape(b, lc, DIFF_HEADS, 2, DIFF_DK)
    diff_c = diff_output(diff_attend(dq_ctx, dk_ctx, dv_ctx, lam), diff_gn_w, lam_init)
    conv_c = conformer_conv(cv_c, cg_c, conv_w, conv_b, conv_ln_w, conv_ln_b)
    y_c = (jnp.concatenate([ret_c, diff_c, conv_c], axis=-1) @ w_out).astype(hc.dtype)
    return y_x, y_c


def setup_inputs(seed: int = 0) -> dict:
    key = jax.random.key(seed)
    ks = jax.random.split(key, 24)
    nrm = lambda k, shape, s: jax.random.normal(k, shape, jnp.float32) * s
    gain = lambda k, shape: 1.0 + 0.1 * jax.random.normal(k, shape, jnp.float32)
    base_decay = jnp.log(-jnp.log(1.0 - 2.0 ** (-5.0 - jnp.arange(RET_HEADS, dtype=jnp.float32))))
    return {
        'x': nrm(ks[0], (BATCH, SEQ, D_MODEL), 1.0),
        'c': nrm(ks[1], (BATCH, D_MODEL), 1.0),
        'ctx': nrm(ks[2], (BATCH, CTX_LEN, D_MODEL), 1.0),
        'c_ctx': nrm(ks[3], (D_MODEL,), 1.0),
        'norm1_w': gain(ks[4], (DEPTH, D_MODEL)),
        'norm2_w': gain(ks[5], (DEPTH, D_MODEL)),
        'ada_w': nrm(ks[6], (DEPTH, D_MODEL, N_MOD * D_MODEL), 0.5 * D_MODEL ** -0.5),
        'ada_b': nrm(ks[7], (DEPTH, N_MOD * D_MODEL), 0.01),
        'w_in': nrm(ks[8], (DEPTH, D_MODEL, IN_WIDTH), D_MODEL ** -0.5),
        'ret_log_decay': base_decay[None, None, :] + nrm(ks[9], (DEPTH, 2, RET_HEADS), 0.05),
        'ret_gn_w': gain(ks[10], (DEPTH, RET_WIDTH)),
        'diff_lam': nrm(ks[11], (DEPTH, 4, DIFF_DK), 0.1),
        'diff_gn_w': gain(ks[12], (DEPTH, DIFF_WIDTH)),
        'conv_w': nrm(ks[13], (DEPTH, CONV_WIDTH, CONV_CH), CONV_WIDTH ** -0.5),
        'conv_b': nrm(ks[14], (DEPTH, CONV_CH), 0.01),
        'conv_ln_w': gain(ks[15], (DEPTH, CONV_CH)),
        'conv_ln_b': nrm(ks[16], (DEPTH, CONV_CH), 0.01),
        'w_out': nrm(ks[17], (DEPTH, MIX_WIDTH, D_MODEL), MIX_WIDTH ** -0.5),
        'mlp_w1': nrm(ks[18], (DEPTH, D_MODEL, D_FF), D_MODEL ** -0.5),
        'mlp_w2': nrm(ks[19], (DEPTH, D_FF, D_MODEL), D_FF ** -0.5),
        'final_norm_w': gain(ks[20], (D_MODEL,)),
    }


def reference(x, c, ctx, c_ctx, norm1_w, norm2_w, ada_w, ada_b, w_in, ret_log_decay, ret_gn_w,
              diff_lam, diff_gn_w, conv_w, conv_b, conv_ln_w, conv_ln_b, w_out, mlp_w1, mlp_w2,
              final_norm_w):
    l = x.shape[1]
    rows_n = l // GRID_W
    rows = jnp.repeat(jnp.arange(rows_n), GRID_W)
    cols = jnp.tile(jnp.arange(GRID_W), rows_n)
    silu_c = jax.nn.silu(c)
    silu_cc = jax.nn.silu(c_ctx)
    for i in range(DEPTH):
        last = i == DEPTH - 1
        lam_init = 0.8 - 0.6 * math.exp(-0.3 * i)
        mod_x = jnp.split((silu_c @ ada_w[i] + ada_b[i])[:, None, :], N_MOD, axis=-1)
        mod_c = jnp.split(silu_cc @ ada_w[i] + ada_b[i], N_MOD, axis=-1)
        hx = modulate(rmsnorm(x, norm1_w[i]), mod_x[0], mod_x[1])
        hc = modulate(rmsnorm(ctx, norm1_w[i]), mod_c[0], mod_c[1])
        y_x, y_c = hybrid_mixer(hx, hc, w_in[i], ret_log_decay[i], ret_gn_w[i], diff_lam[i], lam_init,
                                diff_gn_w[i], conv_w[i], conv_b[i], conv_ln_w[i], conv_ln_b[i], w_out[i],
                                rows, cols, not last)
        x = x + mod_x[2] * y_x
        hx2 = modulate(rmsnorm(x, norm2_w[i]), mod_x[3], mod_x[4])
        x = x + mod_x[5] * sq_relu_mlp(hx2, mlp_w1[i], mlp_w2[i])
        if not last:
            ctx = ctx + mod_c[2] * y_c
            hc2 = modulate(rmsnorm(ctx, norm2_w[i]), mod_c[3], mod_c[4])
            ctx = ctx + mod_c[5] * sq_relu_mlp(hc2, mlp_w1[i], mlp_w2[i])
    return rmsnorm(x, final_norm_w)
```

```python
import functools
import math

import numpy as np
import jax
import jax.numpy as jnp
from jax import lax
from jax.experimental import pallas as pl
from jax.experimental.pallas import tpu as pltpu

D_MODEL = 1024
BATCH = 16
SEQ = 2048
DEPTH = 2
CTX_LEN = 256
GRID_W = 64
EPS = 1e-6
ROPE_BASE = 10000.0
RET_HEADS = 4
RET_DK = 64
RET_DV = 64
RET_QK = RET_HEADS * RET_DK
RET_WIDTH = RET_HEADS * RET_DV
DIFF_HEADS = 4
DIFF_DK = 64
DIFF_DV = 2 * DIFF_DK
DIFF_QK = DIFF_HEADS * 2 * DIFF_DK
DIFF_WIDTH = DIFF_HEADS * DIFF_DV
CONV_CH = D_MODEL // 4
CONV_WIDTH = 31
CONV_PAD = (CONV_WIDTH - 1) // 2
MIX_WIDTH = RET_WIDTH + DIFF_WIDTH + CONV_CH
IN_WIDTH = 2 * RET_QK + 2 * RET_WIDTH + 2 * DIFF_QK + DIFF_WIDTH + 2 * CONV_CH
D_FF = 4 * D_MODEL
N_MOD = 6

TOK = SEQ + CTX_LEN
LANES = 128
TM = 256
N_TILES = TOK // TM
LAT_TILES = SEQ // TM
CHUNK = 128
N_CHUNKS = TOK // CHUNK
LAT_CHUNKS = SEQ // CHUNK
MOD_ROWS = 24
VMEM_LIMIT = 56 * 1024 * 1024

OFF_RQ = 0
OFF_RK = OFF_RQ + RET_QK
OFF_RV = OFF_RK + RET_QK
OFF_RG = OFF_RV + RET_WIDTH
OFF_DQ = OFF_RG + RET_WIDTH
OFF_DK = OFF_DQ + DIFF_QK
OFF_DV = OFF_DK + DIFF_QK
OFF_CV = OFF_DV + DIFF_WIDTH
OFF_CG = OFF_CV + CONV_CH

F32 = jnp.float32
BF16 = jnp.bfloat16


def _params(n_axes):
    return pltpu.CompilerParams(dimension_semantics=("arbitrary",) * n_axes,
                                vmem_limit_bytes=VMEM_LIMIT)


def _sigmoid(x):
    return 1.0 / (1.0 + jnp.exp(-x))


def _silu(x):
    return x * _sigmoid(x)


def _in_proj_permutation():
    perm = np.arange(IN_WIDTH)
    ret = np.empty(RET_QK, np.int64)
    for n in range(RET_QK):
        half, h, i = n // LANES, (n % LANES) // 32, n % 32
        ret[n] = RET_DK * h + 32 * half + i
    perm[OFF_RQ:OFF_RQ + RET_QK] = OFF_RQ + ret
    perm[OFF_RK:OFF_RK + RET_QK] = OFF_RK + ret
    dif = np.empty(DIFF_QK, np.int64)
    for n in range(DIFF_QK):
        hp, r = n // 256, n % 256
        part, gl, i = r // LANES, (r % LANES) // 32, r % 32
        h, m = 2 * hp + gl // 2, gl % 2
        d = (16 * part + i) if i < 16 else (32 + 16 * part + (i - 16))
        dif[n] = 2 * DIFF_DK * h + DIFF_DK * m + d
    perm[OFF_DQ:OFF_DQ + DIFF_QK] = OFF_DQ + dif
    perm[OFF_DK:OFF_DK + DIFF_QK] = OFF_DK + dif
    return perm


def _rope_tables():
    def angles(pos, dim):
        inv = ROPE_BASE ** (-jnp.arange(0, dim, 2, dtype=F32) / dim)
        return pos.astype(F32)[:, None] * inv[None, :]

    t = jnp.arange(SEQ)
    ang_r = jnp.tile(angles(t, RET_DK), (1, RET_HEADS))
    half = DIFF_DK // 2
    ang_d = jnp.concatenate([angles(t // GRID_W, half), angles(t % GRID_W, half)], axis=1)
    ang_d = jnp.tile(ang_d, (1, 4))

    def table(fn, ang, ctx_value):
        return jnp.concatenate([fn(ang), jnp.full((CTX_LEN, LANES), ctx_value, F32)], axis=0)

    return (table(jnp.cos, ang_r, 1.0), table(jnp.sin, ang_r, 0.0),
            table(jnp.cos, ang_d, 1.0), table(jnp.sin, ang_d, 0.0))


ADA_TN = 1536


def _ada_kernel(cc_ref, w_ref, b_ref, o_ref):
    s = _silu(cc_ref[...]).astype(BF16)
    o_ref[...] = jnp.dot(s, w_ref[...].astype(BF16), preferred_element_type=F32) + b_ref[...]


def _ada(cc, ada_w, ada_b):
    n = N_MOD * D_MODEL
    return pl.pallas_call(
        _ada_kernel,
        grid=(DEPTH, n // ADA_TN),
        in_specs=[
            pl.BlockSpec((MOD_ROWS, D_MODEL), lambda i, j: (0, 0)),
            pl.BlockSpec((None, D_MODEL, ADA_TN), lambda i, j: (i, 0, j)),
            pl.BlockSpec((None, 1, ADA_TN), lambda i, j: (i, 0, j)),
        ],
        out_specs=pl.BlockSpec((None, MOD_ROWS, ADA_TN), lambda i, j: (i, 0, j)),
        out_shape=jax.ShapeDtypeStruct((DEPTH, MOD_ROWS, n), F32),
        compiler_params=_params(2),
        name="ada_mod",
    )(cc, ada_w, ada_b.reshape(DEPTH, 1, n))


def _mod_row(t, b):
    return jnp.where(t >= LAT_TILES, BATCH, b)


def _rmsnorm_mod(x, nw, shift, scale):
    ms = jnp.mean(x * x, axis=-1, keepdims=True)
    y = x * lax.rsqrt(ms + EPS) * nw
    return y * (1.0 + scale) + shift


def _inproj_kernel(x_ref, mod_ref, nw_ref, w_ref, cr_ref, sr_ref, cd_ref, sd_ref, p_ref):
    d = D_MODEL
    h = _rmsnorm_mod(x_ref[...], nw_ref[...], mod_ref[:, 0:d], mod_ref[:, d:2 * d]).astype(BF16)

    def proj(off, width):
        return jnp.dot(h, w_ref[:, off:off + width], preferred_element_type=F32)

    def rope_store(off, cos, sin, mult):
        z = proj(off, 2 * LANES)
        za, zb = z[:, :LANES], z[:, LANES:]
        p_ref[:, off:off + LANES] = ((za * cos - zb * sin) * mult).astype(BF16)
        p_ref[:, off + LANES:off + 2 * LANES] = ((zb * cos + za * sin) * mult).astype(BF16)

    cr, sr, cd, sd = cr_ref[...], sr_ref[...], cd_ref[...], sd_ref[...]
    rope_store(OFF_RQ, cr, sr, 1.0)
    rope_store(OFF_RK, cr, sr, RET_DK ** -0.5)
    for g in range(DIFF_QK // (2 * LANES)):
        rope_store(OFF_DQ + 2 * LANES * g, cd, sd, DIFF_DK ** -0.5)
        rope_store(OFF_DK + 2 * LANES * g, cd, sd, 1.0)
    for off, width in ((OFF_RV, 2 * RET_WIDTH), (OFF_DV, DIFF_WIDTH), (OFF_CV, 2 * CONV_CH)):
        p_ref[:, off:off + width] = proj(off, width).astype(BF16)


def _inproj(xs, mod, nw, w_bf, tables):
    const = lambda shape: pl.BlockSpec(shape, lambda t, b: (0,) * len(shape))
    tab = pl.BlockSpec((TM, LANES), lambda t, b: (t, 0))
    return pl.pallas_call(
        _inproj_kernel,
        grid=(N_TILES, BATCH),
        in_specs=[
            pl.BlockSpec((None, TM, D_MODEL), lambda t, b: (b, t, 0)),
            pl.BlockSpec((None, 1, N_MOD * D_MODEL), lambda t, b: (_mod_row(t, b), 0, 0)),
            const((1, D_MODEL)),
            const((D_MODEL, IN_WIDTH)),
            tab, tab, tab, tab,
        ],
        out_specs=pl.BlockSpec((None, TM, IN_WIDTH), lambda t, b: (b, t, 0)),
        out_shape=jax.ShapeDtypeStruct((BATCH, TOK, IN_WIDTH), BF16),
        compiler_params=_params(2),
        name="in_proj",
    )(xs, mod, nw, w_bf, *tables)


def _ret_kernel(ld_ref, q_ref, k_ref, v_ref, g_ref, gnw_ref, o_ref,
                oacc, state, dtab, qdtab, kdtab, cdtab):
    c = CHUNK
    w = RET_QK
    lane = lax.broadcasted_iota(jnp.int32, (c, w), 1)
    row = lax.broadcasted_iota(jnp.int32, (c, w), 0).astype(F32)
    hq = (lane % LANES) // 32
    hv = lane // RET_DV
    ri = lax.broadcasted_iota(jnp.int32, (c, c), 0)
    ci = lax.broadcasted_iota(jnp.int32, (c, c), 1)

    for d in range(2):
        lgq = jnp.zeros((c, w), F32)
        lgv = jnp.zeros((c, w), F32)
        for h in range(RET_HEADS):
            lg = -jnp.exp(jnp.full((c, w), ld_ref[d, h], F32))
            lgq = jnp.where(hq == h, lg, lgq)
            lgv = jnp.where(hv == h, lg, lgv)
            lgm = -jnp.exp(jnp.full((c, c), ld_ref[d, h], F32))
            rel = (ri - ci) if d == 0 else (ci - ri)
            dtab[d, h] = jnp.where(rel >= 0, jnp.exp(lgm * jnp.maximum(rel, 0).astype(F32)), 0.0)
        if d == 0:
            qdtab[d] = jnp.exp(lgq * (row + 1.0))
            kdtab[d] = jnp.exp(lgq * (c - 1.0 - row))
        else:
            qdtab[d] = jnp.exp(lgq * (c - row))
            kdtab[d] = jnp.exp(lgq * row)
        cdtab[d] = jnp.exp(lgv[0:8, :] * float(c))

    srow = lax.broadcasted_iota(jnp.int32, (w, w), 0)
    scol = lax.broadcasted_iota(jnp.int32, (w, w), 1)
    same_head = ((srow % LANES) // 32) == (scol // RET_DV)
    group_avg = jnp.where((srow // RET_DV) == (scol // RET_DV), 1.0 / RET_DV, 0.0).astype(BF16)

    def chunk_out(d, r0):
        qc = q_ref[pl.ds(r0, c), :]
        kc = k_ref[pl.ds(r0, c), :]
        vc = v_ref[pl.ds(r0, c), :]
        s_prev = state[...]
        qd = (qc.astype(F32) * qdtab[d]).astype(BF16)
        o = jnp.dot(qd, s_prev.astype(BF16), preferred_element_type=F32)
        for h in range(RET_HEADS):
            qm = jnp.where(hq == h, qc, jnp.zeros_like(qc))
            s = lax.dot_general(qm, kc, (((1,), (1,)), ((), ())), preferred_element_type=F32)
            s = s * dtab[d, h]
            vm = jnp.where(hv == h, vc, jnp.zeros_like(vc))
            o = o + jnp.dot(s.astype(BF16), vm, preferred_element_type=F32)
        kd = (kc.astype(F32) * kdtab[d]).astype(BF16)
        kv = lax.dot_general(kd, vc, (((0,), (0,)), ((), ())), preferred_element_type=F32)
        state[...] = s_prev * cdtab[d][0:1, :] + jnp.where(same_head, kv, 0.0)
        return o

    def hi_lo_dot(a, m):
        hi = a.astype(BF16)
        lo = (a - hi.astype(F32)).astype(BF16)
        return (jnp.dot(hi, m, preferred_element_type=F32)
                + jnp.dot(lo, m, preferred_element_type=F32))

    state[...] = jnp.zeros_like(state)

    def fwd(i, carry):
        ch = jnp.where(i < N_CHUNKS - LAT_CHUNKS, i + LAT_CHUNKS, i - (N_CHUNKS - LAT_CHUNKS))
        r0 = pl.multiple_of(ch * c, c)
        oacc[pl.ds(r0, c), :] = chunk_out(0, r0)
        return carry

    lax.fori_loop(0, N_CHUNKS, fwd, 0)

    state[...] = jnp.zeros_like(state)

    def bwd(i, carry):
        ch = N_CHUNKS - 1 - i
        r0 = pl.multiple_of(ch * c, c)
        o = oacc[pl.ds(r0, c), :] + chunk_out(1, r0)
        mu = hi_lo_dot(o, group_avg)
        dev = o - mu
        var = hi_lo_dot(dev * dev, group_avg)
        t = dev * lax.rsqrt(var + EPS) * gnw_ref[...]
        g = g_ref[pl.ds(r0, c), :].astype(F32)
        o_ref[pl.ds(r0, c), :] = (t * _silu(g)).astype(BF16)
        return carry

    lax.fori_loop(0, N_CHUNKS, bwd, 0)


def _retention(p, ret_log_decay, gn_w):
    w = RET_QK
    col = lambda j: pl.BlockSpec((None, TOK, w), lambda b: (b, 0, j))
    return pl.pallas_call(
        _ret_kernel,
        grid=(BATCH,),
        in_specs=[
            pl.BlockSpec(memory_space=pltpu.SMEM),
            col(OFF_RQ // w), col(OFF_RK // w), col(OFF_RV // w), col(OFF_RG // w),
            pl.BlockSpec((1, w), lambda b: (0, 0)),
        ],
        out_specs=pl.BlockSpec((None, TOK, w), lambda b: (b, 0, 0)),
        out_shape=jax.ShapeDtypeStruct((BATCH, TOK, w), BF16),
        scratch_shapes=[
            pltpu.VMEM((TOK, w), F32),
            pltpu.VMEM((w, w), F32),
            pltpu.VMEM((2, RET_HEADS, CHUNK, CHUNK), F32),
            pltpu.VMEM((2, CHUNK, w), F32),
            pltpu.VMEM((2, CHUNK, w), F32),
            pltpu.VMEM((2, 8, w), F32),
        ],
        compiler_params=_params(1),
        name="retention",
    )(ret_log_decay, p, p, p, p, gn_w)


def _diff_kernel(q_ref, k_ref, v_ref, lam_ref, gnw_ref, o_ref, *, lam_init, has_ctx):
    lp = lam_ref[...]
    lam = (jnp.exp(jnp.sum(lp[0:1] * lp[1:2], axis=-1, keepdims=True))
           - jnp.exp(jnp.sum(lp[2:3] * lp[3:4], axis=-1, keepdims=True)) + lam_init)
    lane = lax.broadcasted_iota(jnp.int32, (TM, 2 * LANES), 1)
    grp = (lane % LANES) // 32

    def attend(k0, klen):
        for hp in range(DIFF_HEADS // 2):
            cols = slice(2 * LANES * hp, 2 * LANES * (hp + 1))
            qh = q_ref[:, cols]
            kh = k_ref[k0:k0 + klen, cols]
            for hl in range(2):
                h = 2 * hp + hl
                probs = []
                for m in range(2):
                    qm = jnp.where(grp == 2 * hl + m, qh, jnp.zeros_like(qh))
                    s = lax.dot_general(qm, kh, (((1,), (1,)), ((), ())), preferred_element_type=F32)
                    e = jnp.exp(s - jnp.max(s, axis=-1, keepdims=True))
                    probs.append(e / jnp.sum(e, axis=-1, keepdims=True))
                a = (probs[0] - lam * probs[1]).astype(BF16)
                vh = v_ref[k0:k0 + klen, DIFF_DV * h:DIFF_DV * (h + 1)]
                o = jnp.dot(a, vh, preferred_element_type=F32)
                ms = jnp.mean(o * o, axis=-1, keepdims=True)
                y = o * lax.rsqrt(ms + EPS) * gnw_ref[:, DIFF_DV * h:DIFF_DV * (h + 1)]
                o_ref[:, DIFF_DV * h:DIFF_DV * (h + 1)] = (y * (1.0 - lam_init)).astype(BF16)

    if has_ctx:
        t = pl.program_id(1)

        @pl.when(t < LAT_TILES)
        def _():
            attend(0, TOK)

        @pl.when(t >= LAT_TILES)
        def _():
            attend(SEQ, CTX_LEN)
    else:
        attend(0, TOK)


def _diff_attention(p, diff_lam, gn_w, lam_init, has_ctx):
    w = DIFF_QK
    n_t = N_TILES if has_ctx else LAT_TILES
    return pl.pallas_call(
        functools.partial(_diff_kernel, lam_init=lam_init, has_ctx=has_ctx),
        grid=(BATCH, n_t),
        in_specs=[
            pl.BlockSpec((None, TM, w), lambda b, t: (b, t, OFF_DQ // w)),
            pl.BlockSpec((None, TOK, w), lambda b, t: (b, 0, OFF_DK // w)),
            pl.BlockSpec((None, TOK, w), lambda b, t: (b, 0, OFF_DV // w)),
            pl.BlockSpec((4, DIFF_DK), lambda b, t: (0, 0)),
            pl.BlockSpec((1, w), lambda b, t: (0, 0)),
        ],
        out_specs=pl.BlockSpec((None, TM, w), lambda b, t: (b, t, 0)),
        out_shape=jax.ShapeDtypeStruct((BATCH, TOK, w), BF16),
        compiler_params=_params(2),
        name="diff_attn",
    )(p, p, p, diff_lam, gn_w)


CONV_GAP = 16
CONV_LAT0 = CONV_GAP
CONV_CTX0 = CONV_LAT0 + SEQ + CONV_GAP
CONV_ROWS = CONV_CTX0 + CTX_LEN + CONV_GAP
CONV_TILE = 128


def _conv_kernel(cv_ref, cg_ref, w_ref, b_ref, lnw_ref, lnb_ref, o_ref, ubuf):
    ch = CONV_CH
    zeros = jnp.zeros((CONV_GAP, ch), F32)
    ubuf[0:CONV_GAP, :] = zeros
    ubuf[CONV_CTX0 - CONV_GAP:CONV_CTX0, :] = zeros
    ubuf[CONV_ROWS - CONV_GAP:CONV_ROWS, :] = zeros
    for src, dst, n in ((0, CONV_LAT0, SEQ), (SEQ, CONV_CTX0, CTX_LEN)):
        val = cv_ref[src:src + n, :].astype(F32)
        gate = cg_ref[src:src + n, :].astype(F32)
        ubuf[dst:dst + n, :] = val * _sigmoid(gate)

    for src, dst, n in ((0, CONV_LAT0, SEQ), (SEQ, CONV_CTX0, CTX_LEN)):
        for r in range(0, n, CONV_TILE):
            base = dst + r - CONV_PAD
            acc = jnp.zeros((CONV_TILE, ch), F32)
            for j in range(CONV_WIDTH):
                acc = acc + ubuf[base + j:base + j + CONV_TILE, :] * w_ref[j:j + 1, :]
            y = acc + b_ref[...]
            mu = jnp.mean(y, axis=-1, keepdims=True)
            dev = y - mu
            var = jnp.mean(dev * dev, axis=-1, keepdims=True)
            z = dev * lax.rsqrt(var + EPS) * lnw_ref[...] + lnb_ref[...]
            o_ref[src + r:src + r + CONV_TILE, :] = _silu(z).astype(BF16)


def _conformer_conv(p, conv_w, conv_b, ln_w, ln_b):
    ch = CONV_CH
    vec = pl.BlockSpec((1, ch), lambda b: (0, 0))
    return pl.pallas_call(
        _conv_kernel,
        grid=(BATCH,),
        in_specs=[
            pl.BlockSpec((None, TOK, ch), lambda b: (b, 0, OFF_CV // ch)),
            pl.BlockSpec((None, TOK, ch), lambda b: (b, 0, OFF_CG // ch)),
            pl.BlockSpec((CONV_WIDTH, ch), lambda b: (0, 0)),
            vec, vec, vec,
        ],
        out_specs=pl.BlockSpec((None, TOK, ch), lambda b: (b, 0, 0)),
        out_shape=jax.ShapeDtypeStruct((BATCH, TOK, ch), BF16),
        scratch_shapes=[pltpu.VMEM((CONV_ROWS, ch), F32)],
        compiler_params=_params(1),
        name="conformer_conv",
    )(p, p, conv_w, conv_b, ln_w, ln_b)


def _out_mlp_kernel(ret_ref, dif_ref, cnv_ref, x_ref, mod_ref, nw_ref, wo_ref, w1_ref, w2_ref,
                    fin_ref, o_ref, *, final):
    d = D_MODEL
    y = jnp.dot(ret_ref[...], wo_ref[0:RET_WIDTH, :], preferred_element_type=F32)
    y = y + jnp.dot(dif_ref[...], wo_ref[RET_WIDTH:RET_WIDTH + DIFF_WIDTH, :],
                    preferred_element_type=F32)
    y = y + jnp.dot(cnv_ref[...], wo_ref[RET_WIDTH + DIFF_WIDTH:MIX_WIDTH, :],
                    preferred_element_type=F32)
    x1 = x_ref[...] + mod_ref[:, 2 * d:3 * d] * y
    h2 = _rmsnorm_mod(x1, nw_ref[...], mod_ref[:, 3 * d:4 * d], mod_ref[:, 4 * d:5 * d]).astype(BF16)
    u = jnp.maximum(jnp.dot(h2, w1_ref[...], preferred_element_type=F32), 0.0)
    z = jnp.dot((u * u).astype(BF16), w2_ref[...], preferred_element_type=F32)
    x2 = x1 + mod_ref[:, 5 * d:6 * d] * z
    if final:
        ms = jnp.mean(x2 * x2, axis=-1, keepdims=True)
        x2 = x2 * lax.rsqrt(ms + EPS) * fin_ref[...]
    o_ref[...] = x2


def _out_mlp(ret, dif, cnv, xs, mod, nw, wo, w1, w2, fin, final):
    n_t = LAT_TILES if final else N_TILES
    rows = SEQ if final else TOK
    const = lambda shape: pl.BlockSpec(shape, lambda b, t: (0,) * len(shape),
                                       pipeline_mode=pl.Buffered(1))
    tile = lambda width: pl.BlockSpec((None, TM, width), lambda b, t: (b, t, 0))
    return pl.pallas_call(
        functools.partial(_out_mlp_kernel, final=final),
        grid=(BATCH, n_t),
        in_specs=[
            tile(RET_WIDTH), tile(DIFF_WIDTH), tile(CONV_CH), tile(D_MODEL),
            pl.BlockSpec((None, 1, N_MOD * D_MODEL), lambda b, t: (_mod_row(t, b), 0, 0)),
            const((1, D_MODEL)),
            const((MIX_WIDTH, D_MODEL)), const((D_MODEL, D_FF)), const((D_FF, D_MODEL)),
            const((1, D_MODEL)),
        ],
        out_specs=tile(D_MODEL),
        out_shape=jax.ShapeDtypeStruct((BATCH, rows, D_MODEL), F32),
        compiler_params=_params(2),
        name="out_mlp",
    )(ret, dif, cnv, xs, mod, nw, wo, w1, w2, fin)


def kernel(x, c, ctx, c_ctx, norm1_w, norm2_w, ada_w, ada_b, w_in, ret_log_decay, ret_gn_w, diff_lam, diff_gn_w, conv_w, conv_b, conv_ln_w, conv_ln_b, w_out, mlp_w1, mlp_w2, final_norm_w):
    assert x.shape == (BATCH, SEQ, D_MODEL) and ctx.shape == (BATCH, CTX_LEN, D_MODEL)
    perm = _in_proj_permutation()
    tables = _rope_tables()
    xs = jnp.concatenate([x, ctx], axis=1)
    cc = jnp.concatenate([c, c_ctx[None, :], jnp.zeros((MOD_ROWS - BATCH - 1, D_MODEL), F32)], axis=0)
    mod_all = _ada(cc, ada_w, ada_b).reshape(DEPTH, MOD_ROWS, 1, N_MOD * D_MODEL)
    fin = final_norm_w.reshape(1, D_MODEL)
    for i in range(DEPTH):
        last = i == DEPTH - 1
        lam_init = 0.8 - 0.6 * math.exp(-0.3 * i)
        w_bf = w_in[i][:, perm].astype(BF16)
        p = _inproj(xs, mod_all[i], norm1_w[i].reshape(1, D_MODEL), w_bf, tables)
        ret = _retention(p, ret_log_decay[i], ret_gn_w[i].reshape(1, RET_WIDTH))
        dif = _diff_attention(p, diff_lam[i], diff_gn_w[i].reshape(1, DIFF_WIDTH), lam_init, not last)
        cnv = _conformer_conv(p, conv_w[i], conv_b[i].reshape(1, CONV_CH),
                              conv_ln_w[i].reshape(1, CONV_CH), conv_ln_b[i].reshape(1, CONV_CH))
        xs = _out_mlp(ret, dif, cnv, xs, mod_all[i], norm2_w[i].reshape(1, D_MODEL),
                      w_out[i].astype(BF16), mlp_w1[i].astype(BF16), mlp_w2[i].astype(BF16),
                      fin, last)
    return xs
```

```python
import functools
import math

import numpy as np
import jax
import jax.numpy as jnp
from jax import lax
from jax.experimental import pallas as pl
from jax.experimental.pallas import tpu as pltpu

D_MODEL = 1024
BATCH = 16
SEQ = 2048
DEPTH = 2
CTX_LEN = 256
GRID_W = 64
EPS = 1e-6
ROPE_BASE = 10000.0
RET_HEADS = 4
RET_DK = 64
RET_DV = 64
RET_QK = RET_HEADS * RET_DK
RET_WIDTH = RET_HEADS * RET_DV
DIFF_HEADS = 4
DIFF_DK = 64
DIFF_DV = 2 * DIFF_DK
DIFF_QK = DIFF_HEADS * 2 * DIFF_DK
DIFF_WIDTH = DIFF_HEADS * DIFF_DV
CONV_CH = D_MODEL // 4
CONV_WIDTH = 31
CONV_PAD = (CONV_WIDTH - 1) // 2
MIX_WIDTH = RET_WIDTH + DIFF_WIDTH + CONV_CH
IN_WIDTH = 2 * RET_QK + 2 * RET_WIDTH + 2 * DIFF_QK + DIFF_WIDTH + 2 * CONV_CH
D_FF = 4 * D_MODEL
N_MOD = 6

TOK = SEQ + CTX_LEN
LANES = 128
TM = 256
N_TILES = TOK // TM
LAT_TILES = SEQ // TM
MOD_ROWS = 24
VMEM_LIMIT = 56 * 1024 * 1024

OFF_RQ = 0
OFF_RK = OFF_RQ + RET_QK
OFF_RV = OFF_RK + RET_QK
OFF_RG = OFF_RV + RET_WIDTH
OFF_DQ = OFF_RG + RET_WIDTH
OFF_DK = OFF_DQ + DIFF_QK
OFF_DV = OFF_DK + DIFF_QK
OFF_CV = OFF_DV + DIFF_WIDTH
OFF_CG = OFF_CV + CONV_CH

F32 = jnp.float32
BF16 = jnp.bfloat16
LOG2_E = math.log2(math.e)


def _params(n_axes):
    return pltpu.CompilerParams(dimension_semantics=("arbitrary",) * n_axes,
                                vmem_limit_bytes=VMEM_LIMIT)


def _sigmoid(x):
    return 1.0 / (1.0 + jnp.exp(-x))


def _silu(x):
    return x * _sigmoid(x)


def _in_proj_permutation():
    perm = np.arange(IN_WIDTH)
    ret = np.empty(RET_QK, np.int64)
    for n in range(RET_QK):
        half, h, i = n // LANES, (n % LANES) // 32, n % 32
        ret[n] = RET_DK * h + 32 * half + i
    perm[OFF_RQ:OFF_RQ + RET_QK] = OFF_RQ + ret
    perm[OFF_RK:OFF_RK + RET_QK] = OFF_RK + ret
    dif = np.empty(DIFF_QK, np.int64)
    for n in range(DIFF_QK):
        hp, r = n // 256, n % 256
        part, gl, i = r // LANES, (r % LANES) // 32, r % 32
        h, m = 2 * hp + gl // 2, gl % 2
        d = (16 * part + i) if i < 16 else (32 + 16 * part + (i - 16))
        dif[n] = 2 * DIFF_DK * h + DIFF_DK * m + d
    perm[OFF_DQ:OFF_DQ + DIFF_QK] = OFF_DQ + dif
    perm[OFF_DK:OFF_DK + DIFF_QK] = OFF_DK + dif
    return perm


def _rope_tables():
    def angles(pos, dim):
        inv = ROPE_BASE ** (-jnp.arange(0, dim, 2, dtype=F32) / dim)
        return pos.astype(F32)[:, None] * inv[None, :]

    t = jnp.arange(SEQ)
    ang_r = jnp.tile(angles(t, RET_DK), (1, RET_HEADS))
    half = DIFF_DK // 2
    ang_d = jnp.concatenate([angles(t // GRID_W, half), angles(t % GRID_W, half)], axis=1)
    ang_d = jnp.tile(ang_d, (1, 4))

    def table(fn, ang, ctx_value):
        return jnp.concatenate([fn(ang), jnp.full((CTX_LEN, LANES), ctx_value, F32)], axis=0)

    return (table(jnp.cos, ang_r, 1.0), table(jnp.sin, ang_r, 0.0),
            table(jnp.cos, ang_d, 1.0), table(jnp.sin, ang_d, 0.0))


ADA_TN = 1536


def _ada_kernel(cc_ref, w_ref, b_ref, o_ref):
    s = _silu(cc_ref[...]).astype(BF16)
    o_ref[...] = jnp.dot(s, w_ref[...].astype(BF16), preferred_element_type=F32) + b_ref[...]


def _ada(cc, ada_w, ada_b):
    n = N_MOD * D_MODEL
    return pl.pallas_call(
        _ada_kernel,
        grid=(DEPTH, n // ADA_TN),
        in_specs=[
            pl.BlockSpec((MOD_ROWS, D_MODEL), lambda i, j: (0, 0)),
            pl.BlockSpec((None, D_MODEL, ADA_TN), lambda i, j: (i, 0, j)),
            pl.BlockSpec((None, 1, ADA_TN), lambda i, j: (i, 0, j)),
        ],
        out_specs=pl.BlockSpec((None, MOD_ROWS, ADA_TN), lambda i, j: (i, 0, j)),
        out_shape=jax.ShapeDtypeStruct((DEPTH, MOD_ROWS, n), F32),
        compiler_params=_params(2),
        name="ada_mod",
    )(cc, ada_w, ada_b.reshape(DEPTH, 1, n))


def _mod_row(t, b):
    return jnp.where(t >= LAT_TILES, BATCH, b)


def _rmsnorm_mod(x, nw, shift, scale):
    ms = jnp.mean(x * x, axis=-1, keepdims=True)
    y = x * lax.rsqrt(ms + EPS) * nw
    return y * (1.0 + scale) + shift


def _inproj_kernel(x_ref, mod_ref, nw_ref, w_ref, cr_ref, sr_ref, cd_ref, sd_ref, p_ref):
    d = D_MODEL
    h = _rmsnorm_mod(x_ref[...], nw_ref[...], mod_ref[:, 0:d], mod_ref[:, d:2 * d]).astype(BF16)

    def proj(off, width):
        return jnp.dot(h, w_ref[:, off:off + width], preferred_element_type=F32)

    def rope_store(off, cos, sin, mult):
        z = proj(off, 2 * LANES)
        za, zb = z[:, :LANES], z[:, LANES:]
        p_ref[:, off:off + LANES] = ((za * cos - zb * sin) * mult).astype(BF16)
        p_ref[:, off + LANES:off + 2 * LANES] = ((zb * cos + za * sin) * mult).astype(BF16)

    cr, sr, cd, sd = cr_ref[...], sr_ref[...], cd_ref[...], sd_ref[...]
    rope_store(OFF_RQ, cr, sr, 1.0)
    rope_store(OFF_RK, cr, sr, RET_DK ** -0.5)
    for g in range(DIFF_QK // (2 * LANES)):
        rope_store(OFF_DQ + 2 * LANES * g, cd, sd, DIFF_DK ** -0.5 * LOG2_E)
        rope_store(OFF_DK + 2 * LANES * g, cd, sd, 1.0)
    for off, width in ((OFF_RV, 2 * RET_WIDTH), (OFF_DV, DIFF_WIDTH), (OFF_CV, 2 * CONV_CH)):
        p_ref[:, off:off + width] = proj(off, width).astype(BF16)


def _inproj(xs, mod, nw, w_bf, tables):
    const = lambda shape: pl.BlockSpec(shape, lambda t, b: (0,) * len(shape))
    tab = pl.BlockSpec((TM, LANES), lambda t, b: (t, 0))
    return pl.pallas_call(
        _inproj_kernel,
        grid=(N_TILES, BATCH),
        in_specs=[
            pl.BlockSpec((None, TM, D_MODEL), lambda t, b: (b, t, 0)),
            pl.BlockSpec((None, 1, N_MOD * D_MODEL), lambda t, b: (_mod_row(t, b), 0, 0)),
            const((1, D_MODEL)),
            const((D_MODEL, IN_WIDTH)),
            tab, tab, tab, tab,
        ],
        out_specs=pl.BlockSpec((None, TM, IN_WIDTH), lambda t, b: (b, t, 0)),
        out_shape=jax.ShapeDtypeStruct((BATCH, TOK, IN_WIDTH), BF16),
        compiler_params=_params(2),
        name="in_proj",
    )(xs, mod, nw, w_bf, *tables)


RET_C = 256
RET_NC = TOK // RET_C
RET_LAT_NC = SEQ // RET_C
QD_F, QD_B, KD_F, KD_B, CD_F, CD_B = range(6)


def _ret_kernel(ld_ref, q_ref, k_ref, v_ref, g_ref, gnw_ref, o_ref, dtab, dvec, kvbuf, sbuf):
    c = RET_C
    w = RET_QK
    assert c == w
    lane = lax.broadcasted_iota(jnp.int32, (c, w), 1)
    rowi = lax.broadcasted_iota(jnp.int32, (c, w), 0)
    row = rowi.astype(F32)
    hq = (lane % LANES) // 32
    hv = lane // RET_DV
    rel = rowi - lane
    rel_f = jnp.maximum(rel, 0).astype(F32)
    rel_b = jnp.maximum(-rel, 0).astype(F32)

    lgq = [jnp.zeros((c, w), F32), jnp.zeros((c, w), F32)]
    lgv = [jnp.zeros((c, w), F32), jnp.zeros((c, w), F32)]
    for h in range(RET_HEADS):
        lg = [-jnp.exp(jnp.full((c, w), ld_ref[d, h], F32)) for d in range(2)]
        dtab[h * c:(h + 1) * c, :] = jnp.where(
            rel > 0, jnp.exp(lg[0] * rel_f), jnp.where(rel < 0, jnp.exp(lg[1] * rel_b), 2.0))
        for d in range(2):
            lgq[d] = jnp.where(hq == h, lg[d], lgq[d])
            lgv[d] = jnp.where(hv == h, lg[d], lgv[d])
    dvec[QD_F] = jnp.exp(lgq[0] * (row + 1.0))
    dvec[QD_B] = jnp.exp(lgq[1] * (c - row))
    dvec[KD_F] = jnp.exp(lgq[0] * (c - 1.0 - row))
    dvec[KD_B] = jnp.exp(lgq[1] * row)
    dvec[CD_F] = jnp.exp(lgv[0] * float(c))
    dvec[CD_B] = jnp.exp(lgv[1] * float(c))

    same_head = ((rowi % LANES) // 32) == (lane // RET_DV)
    group_avg = jnp.where((rowi // RET_DV) == (lane // RET_DV), 1.0 / RET_DV, 0.0).astype(BF16)

    def hi_lo_dot(a, m):
        hi = a.astype(BF16)
        lo = (a - hi.astype(F32)).astype(BF16)
        return (jnp.dot(hi, m, preferred_element_type=F32)
                + jnp.dot(lo, m, preferred_element_type=F32))

    for n in range(RET_NC):
        rows = slice(n * c, (n + 1) * c)
        kc = k_ref[rows, :].astype(F32)
        k2 = jnp.concatenate([(kc * dvec[KD_F]).astype(BF16), (kc * dvec[KD_B]).astype(BF16)], axis=1)
        kv = lax.dot_general(k2, v_ref[rows, :], (((0,), (0,)), ((), ())),
                             preferred_element_type=F32)
        kvbuf[0, n] = jnp.where(same_head, kv[:w], 0.0)
        kvbuf[1, n] = jnp.where(same_head, kv[w:], 0.0)

    orders = ([RET_LAT_NC] + list(range(RET_LAT_NC)), [RET_LAT_NC] + list(range(RET_LAT_NC - 1, -1, -1)))
    for d, order in enumerate(orders):
        s = jnp.zeros((w, w), F32)
        for n in order[:-1]:
            sbuf[d, n] = s.astype(BF16)
            s = s * dvec[CD_F + d] + kvbuf[d, n]
        sbuf[d, order[-1]] = s.astype(BF16)

    for n in range(RET_NC):
        rows = slice(n * c, (n + 1) * c)
        qc, kc, vc = q_ref[rows, :], k_ref[rows, :], v_ref[rows, :]
        q4 = jnp.concatenate([jnp.where(hq == h, qc, jnp.zeros_like(qc)) for h in range(RET_HEADS)],
                             axis=0)
        s = lax.dot_general(q4, kc, (((1,), (1,)), ((), ())), preferred_element_type=F32)
        pw = (s * dtab[...]).astype(BF16)
        qf = qc.astype(F32)
        lhs = jnp.concatenate([pw[h * c:(h + 1) * c] for h in range(RET_HEADS)]
                              + [(qf * dvec[QD_F]).astype(BF16), (qf * dvec[QD_B]).astype(BF16)], axis=1)
        rhs = jnp.concatenate([jnp.where(hv == h, vc, jnp.zeros_like(vc)) for h in range(RET_HEADS)]
                              + [sbuf[0, n], sbuf[1, n]], axis=0)
        o = jnp.dot(lhs, rhs, preferred_element_type=F32)
        mu = hi_lo_dot(o, group_avg)
        dev = o - mu
        var = hi_lo_dot(dev * dev, group_avg)
        t = dev * lax.rsqrt(var + EPS) * gnw_ref[...]
        g = g_ref[rows, :].astype(F32)
        o_ref[rows, :] = (t * _silu(g)).astype(BF16)


def _retention(p, ret_log_decay, gn_w):
    w = RET_QK
    col = lambda j: pl.BlockSpec((None, TOK, w), lambda b: (b, 0, j))
    return pl.pallas_call(
        _ret_kernel,
        grid=(BATCH,),
        in_specs=[
            pl.BlockSpec(memory_space=pltpu.SMEM),
            col(OFF_RQ // w), col(OFF_RK // w), col(OFF_RV // w), col(OFF_RG // w),
            pl.BlockSpec((1, w), lambda b: (0, 0)),
        ],
        out_specs=pl.BlockSpec((None, TOK, w), lambda b: (b, 0, 0)),
        out_shape=jax.ShapeDtypeStruct((BATCH, TOK, w), BF16),
        scratch_shapes=[
            pltpu.VMEM((RET_HEADS * RET_C, RET_C), F32),
            pltpu.VMEM((6, RET_C, w), F32),
            pltpu.VMEM((2, RET_NC, w, w), F32),
            pltpu.VMEM((2, RET_NC, w, w), BF16),
        ],
        compiler_params=_params(1),
        name="retention",
    )(ret_log_decay, p, p, p, p, gn_w)


DIFF_QT = 512
DIFF_SUB = 256


def _diff_kernel(*refs, lam_init, n_sub, aliased):
    if aliased:
        q_ref, k_ref, v_ref, lam_ref, gnw_ref, _, o_ref, sbuf = refs
    else:
        q_ref, k_ref, v_ref, lam_ref, gnw_ref, o_ref, sbuf = refs
    sub = DIFF_SUB
    lp = lam_ref[...]
    lam = (jnp.exp(jnp.sum(lp[0:1] * lp[1:2], axis=-1, keepdims=True))
           - jnp.exp(jnp.sum(lp[2:3] * lp[3:4], axis=-1, keepdims=True)) + lam_init)
    lane = lax.broadcasted_iota(jnp.int32, (sub, 2 * LANES), 1)
    grp = (lane % LANES) // 32
    items = [(qs, h) for qs in range(n_sub) for h in range(DIFF_HEADS)]

    def scores(idx, slot):
        qs, h = items[idx]
        hp, hl = divmod(h, 2)
        cols = slice(2 * LANES * hp, 2 * LANES * (hp + 1))
        qh = q_ref[qs * sub:(qs + 1) * sub, cols]
        q2 = jnp.concatenate([jnp.where(grp == 2 * hl + m, qh, jnp.zeros_like(qh))
                              for m in range(2)], axis=0)
        s = lax.dot_general(q2, k_ref[:, cols], (((1,), (1,)), ((), ())), preferred_element_type=F32)
        sbuf[slot] = s
        return jnp.max(s, axis=-1, keepdims=True)

    def finish(idx, slot, mx):
        qs, h = items[idx]
        e = jnp.exp2(sbuf[slot] - mx)
        l = jnp.sum(e, axis=-1, keepdims=True)
        sbuf[slot] = e
        l0, l1 = l[:sub], l[sub:]
        a = (sbuf[slot, 0:sub] - (lam * l0 / l1) * sbuf[slot, sub:2 * sub]).astype(BF16)
        hcols = slice(DIFF_DV * h, DIFF_DV * (h + 1))
        o = jnp.dot(a, v_ref[:, hcols], preferred_element_type=F32) / l0
        ms = jnp.mean(o * o, axis=-1, keepdims=True)
        y = o * lax.rsqrt(ms + EPS) * gnw_ref[:, hcols]
        o_ref[qs * sub:(qs + 1) * sub, hcols] = (y * (1.0 - lam_init)).astype(BF16)

    mx = scores(0, 0)
    for idx in range(len(items)):
        mx_next = scores(idx + 1, (idx + 1) % 2) if idx + 1 < len(items) else None
        finish(idx, idx % 2, mx)
        mx = mx_next


def _diff_attention(p, diff_lam, gn_w, lam_init, has_ctx):
    w = DIFF_QK
    small = [pl.BlockSpec((4, DIFF_DK), lambda b, t: (0, 0)), pl.BlockSpec((1, w), lambda b, t: (0, 0))]
    lat = pl.pallas_call(
        functools.partial(_diff_kernel, lam_init=lam_init, n_sub=DIFF_QT // DIFF_SUB, aliased=False),
        grid=(BATCH, SEQ // DIFF_QT),
        in_specs=[
            pl.BlockSpec((None, DIFF_QT, w), lambda b, t: (b, t, OFF_DQ // w)),
            pl.BlockSpec((None, TOK, w), lambda b, t: (b, 0, OFF_DK // w)),
            pl.BlockSpec((None, TOK, w), lambda b, t: (b, 0, OFF_DV // w)),
        ] + small,
        out_specs=pl.BlockSpec((None, DIFF_QT, w), lambda b, t: (b, t, 0)),
        out_shape=jax.ShapeDtypeStruct((BATCH, TOK, w), BF16),
        scratch_shapes=[pltpu.VMEM((2, 2 * DIFF_SUB, TOK), F32)],
        compiler_params=_params(2),
        name="diff_attn",
    )(p, p, p, diff_lam, gn_w)
    if not has_ctx:
        return lat
    ctx_blk = SEQ // CTX_LEN
    ctx_spec = lambda j: pl.BlockSpec((None, CTX_LEN, w), lambda b, t: (b, ctx_blk, j))
    return pl.pallas_call(
        functools.partial(_diff_kernel, lam_init=lam_init, n_sub=CTX_LEN // DIFF_SUB, aliased=True),
        grid=(BATCH, 1),
        in_specs=[ctx_spec(OFF_DQ // w), ctx_spec(OFF_DK // w), ctx_spec(OFF_DV // w)] + small
                 + [pl.BlockSpec(memory_space=pl.ANY)],
        out_specs=ctx_spec(0),
        out_shape=jax.ShapeDtypeStruct((BATCH, TOK, w), BF16),
        scratch_shapes=[pltpu.VMEM((2, 2 * DIFF_SUB, CTX_LEN), F32)],
        input_output_aliases={5: 0},
        compiler_params=_params(2),
        name="diff_attn_ctx",
    )(p, p, p, diff_lam, gn_w, lat)


CONV_GAP = 16
CONV_LAT0 = CONV_GAP
CONV_CTX0 = CONV_LAT0 + SEQ + CONV_GAP
CONV_TILE = 128
SUBLANES = 8
CONV_WIN = CONV_TILE + 2 * SUBLANES
CONV_KW = 256
CONV_ROWS = CONV_CTX0 + CTX_LEN + 2 * CONV_GAP


def _conv_kernel(cv_ref, cg_ref, w_ref, b_ref, lnw_ref, lnb_ref, o_ref, ubuf, zbuf):
    ch = CONV_CH
    ubuf[0:CONV_GAP, :] = jnp.zeros((CONV_GAP, ch), F32)
    ubuf[CONV_CTX0 - CONV_GAP:CONV_CTX0, :] = jnp.zeros((CONV_GAP, ch), F32)
    ubuf[CONV_CTX0 + CTX_LEN:CONV_ROWS, :] = jnp.zeros((2 * CONV_GAP, ch), F32)
    zbuf[...] = jnp.zeros_like(zbuf)
    for src, dst, n in ((0, CONV_LAT0, SEQ), (SEQ, CONV_CTX0, CTX_LEN)):
        val = cv_ref[src:src + n, :].astype(F32)
        gate = cg_ref[src:src + n, :].astype(F32)
        ubuf[dst:dst + n, :] = val * _sigmoid(gate)

    kcol = lax.broadcasted_iota(jnp.int32, (CONV_TILE, SUBLANES * CONV_KW), 1)
    krow = lax.broadcasted_iota(jnp.int32, (CONV_TILE, SUBLANES * CONV_KW), 0)
    sel = jnp.where((kcol % CONV_KW) == krow + kcol // CONV_KW, 1.0, 0.0).astype(BF16)

    for src, dst, n in ((0, CONV_LAT0, SEQ), (SEQ, CONV_CTX0, CTX_LEN)):
        for r0 in range(0, n, CONV_TILE):
            t0 = dst + r0
            for r in range(SUBLANES):
                z = None
                for a in range(-2, 2):
                    o = SUBLANES * a + r
                    if abs(o) > CONV_PAD:
                        continue
                    lo = t0 + SUBLANES * a
                    term = ubuf[lo:lo + CONV_WIN, :] * w_ref[o + CONV_PAD:o + CONV_PAD + 1, :]
                    z = term if z is None else z + term
                zbuf[CONV_KW * r:CONV_KW * r + CONV_WIN, :] = z.astype(BF16)
            y = jnp.dot(sel, zbuf[...], preferred_element_type=F32) + b_ref[...]
            mu = jnp.mean(y, axis=-1, keepdims=True)
            dev = y - mu
            var = jnp.mean(dev * dev, axis=-1, keepdims=True)
            zn = dev * lax.rsqrt(var + EPS) * lnw_ref[...] + lnb_ref[...]
            o_ref[src + r0:src + r0 + CONV_TILE, :] = _silu(zn).astype(BF16)


def _conformer_conv(p, conv_w, conv_b, ln_w, ln_b):
    ch = CONV_CH
    vec = pl.BlockSpec((1, ch), lambda b: (0, 0))
    return pl.pallas_call(
        _conv_kernel,
        grid=(BATCH,),
        in_specs=[
            pl.BlockSpec((None, TOK, ch), lambda b: (b, 0, OFF_CV // ch)),
            pl.BlockSpec((None, TOK, ch), lambda b: (b, 0, OFF_CG // ch)),
            pl.BlockSpec((CONV_WIDTH, ch), lambda b: (0, 0)),
            vec, vec, vec,
        ],
        out_specs=pl.BlockSpec((None, TOK, ch), lambda b: (b, 0, 0)),
        out_shape=jax.ShapeDtypeStruct((BATCH, TOK, ch), BF16),
        scratch_shapes=[pltpu.VMEM((CONV_ROWS, ch), F32),
                        pltpu.VMEM((SUBLANES * CONV_KW, ch), BF16)],
        compiler_params=_params(1),
        name="conformer_conv",
    )(p, p, conv_w, conv_b, ln_w, ln_b)


def _out_mlp_kernel(ret_ref, dif_ref, cnv_ref, x_ref, mod_ref, nw_ref, wo_ref, w1_ref, w2_ref,
                    fin_ref, o_ref, *, final):
    d = D_MODEL
    y = jnp.dot(ret_ref[...], wo_ref[0:RET_WIDTH, :], preferred_element_type=F32)
    y = y + jnp.dot(dif_ref[...], wo_ref[RET_WIDTH:RET_WIDTH + DIFF_WIDTH, :],
                    preferred_element_type=F32)
    y = y + jnp.dot(cnv_ref[...], wo_ref[RET_WIDTH + DIFF_WIDTH:MIX_WIDTH, :],
                    preferred_element_type=F32)
    x1 = x_ref[...] + mod_ref[:, 2 * d:3 * d] * y
    h2 = _rmsnorm_mod(x1, nw_ref[...], mod_ref[:, 3 * d:4 * d], mod_ref[:, 4 * d:5 * d]).astype(BF16)
    u = jnp.maximum(jnp.dot(h2, w1_ref[...], preferred_element_type=F32), 0.0)
    z = jnp.dot((u * u).astype(BF16), w2_ref[...], preferred_element_type=F32)
    x2 = x1 + mod_ref[:, 5 * d:6 * d] * z
    if final:
        ms = jnp.mean(x2 * x2, axis=-1, keepdims=True)
        x2 = x2 * lax.rsqrt(ms + EPS) * fin_ref[...]
    o_ref[...] = x2


def _out_mlp(ret, dif, cnv, xs, mod, nw, wo, w1, w2, fin, final):
    n_t = LAT_TILES if final else N_TILES
    rows = SEQ if final else TOK
    const = lambda shape: pl.BlockSpec(shape, lambda b, t: (0,) * len(shape),
                                       pipeline_mode=pl.Buffered(1))
    tile = lambda width: pl.BlockSpec((None, TM, width), lambda b, t: (b, t, 0))
    return pl.pallas_call(
        functools.partial(_out_mlp_kernel, final=final),
        grid=(BATCH, n_t),
        in_specs=[
            tile(RET_WIDTH), tile(DIFF_WIDTH), tile(CONV_CH), tile(D_MODEL),
            pl.BlockSpec((None, 1, N_MOD * D_MODEL), lambda b, t: (_mod_row(t, b), 0, 0)),
            const((1, D_MODEL)),
            const((MIX_WIDTH, D_MODEL)), const((D_MODEL, D_FF)), const((D_FF, D_MODEL)),
            const((1, D_MODEL)),
        ],
        out_specs=tile(D_MODEL),
        out_shape=jax.ShapeDtypeStruct((BATCH, rows, D_MODEL), F32),
        compiler_params=_params(2),
        name="out_mlp",
    )(ret, dif, cnv, xs, mod, nw, wo, w1, w2, fin)


def kernel(x, c, ctx, c_ctx, norm1_w, norm2_w, ada_w, ada_b, w_in, ret_log_decay, ret_gn_w, diff_lam, diff_gn_w, conv_w, conv_b, conv_ln_w, conv_ln_b, w_out, mlp_w1, mlp_w2, final_norm_w):
    assert x.shape == (BATCH, SEQ, D_MODEL) and ctx.shape == (BATCH, CTX_LEN, D_MODEL)
    perm = _in_proj_permutation()
    tables = _rope_tables()
    xs = jnp.concatenate([x, ctx], axis=1)
    cc = jnp.concatenate([c, c_ctx[None, :], jnp.zeros((MOD_ROWS - BATCH - 1, D_MODEL), F32)], axis=0)
    mod_all = _ada(cc, ada_w, ada_b).reshape(DEPTH, MOD_ROWS, 1, N_MOD * D_MODEL)
    fin = final_norm_w.reshape(1, D_MODEL)
    for i in range(DEPTH):
        last = i == DEPTH - 1
        lam_init = 0.8 - 0.6 * math.exp(-0.3 * i)
        w_bf = w_in[i][:, perm].astype(BF16)
        p = _inproj(xs, mod_all[i], norm1_w[i].reshape(1, D_MODEL), w_bf, tables)
        ret = _retention(p, ret_log_decay[i], ret_gn_w[i].reshape(1, RET_WIDTH))
        dif = _diff_attention(p, diff_lam[i], diff_gn_w[i].reshape(1, DIFF_WIDTH), lam_init, not last)
        cnv = _conformer_conv(p, conv_w[i], conv_b[i].reshape(1, CONV_CH),
                              conv_ln_w[i].reshape(1, CONV_CH), conv_ln_b[i].reshape(1, CONV_CH))
        xs = _out_mlp(ret, dif, cnv, xs, mod_all[i], norm2_w[i].reshape(1, D_MODEL),
                      w_out[i].astype(BF16), mlp_w1[i].astype(BF16), mlp_w2[i].astype(BF16),
                      fin, last)
    return xs
```

```python
import functools
import math

import numpy as np
import jax
import jax.numpy as jnp
from jax import lax
from jax.experimental import pallas as pl
from jax.experimental.pallas import tpu as pltpu

D_MODEL = 1024
BATCH = 16
SEQ = 2048
DEPTH = 2
CTX_LEN = 256
GRID_W = 64
EPS = 1e-6
ROPE_BASE = 10000.0
RET_HEADS = 4
RET_DK = 64
RET_DV = 64
RET_QK = RET_HEADS * RET_DK
RET_WIDTH = RET_HEADS * RET_DV
DIFF_HEADS = 4
DIFF_DK = 64
DIFF_DV = 2 * DIFF_DK
DIFF_QK = DIFF_HEADS * 2 * DIFF_DK
DIFF_WIDTH = DIFF_HEADS * DIFF_DV
CONV_CH = D_MODEL // 4
CONV_WIDTH = 31
CONV_PAD = (CONV_WIDTH - 1) // 2
MIX_WIDTH = RET_WIDTH + DIFF_WIDTH + CONV_CH
IN_WIDTH = 2 * RET_QK + 2 * RET_WIDTH + 2 * DIFF_QK + DIFF_WIDTH + 2 * CONV_CH
D_FF = 4 * D_MODEL
N_MOD = 6

TOK = SEQ + CTX_LEN
LANES = 128
SUBLANES = 8
TM_LAT = 512
TM_CTX = CTX_LEN
MOD_ROWS = 24
VMEM_LIMIT = 56 * 1024 * 1024

OFF_RQ = 0
OFF_RK = OFF_RQ + RET_QK
OFF_RV = OFF_RK + RET_QK
OFF_RG = OFF_RV + RET_WIDTH
OFF_DQ = OFF_RG + RET_WIDTH
OFF_DK = OFF_DQ + DIFF_QK
OFF_DV = OFF_DK + DIFF_QK
OFF_CV = OFF_DV + DIFF_WIDTH
OFF_CG = OFF_CV + CONV_CH

F32 = jnp.float32
BF16 = jnp.bfloat16
LOG2_E = math.log2(math.e)


def _params(n_axes):
    return pltpu.CompilerParams(dimension_semantics=("arbitrary",) * n_axes,
                                vmem_limit_bytes=VMEM_LIMIT)


def _sigmoid(x):
    return 1.0 / (1.0 + jnp.exp(-x))


def _silu(x):
    return x * _sigmoid(x)


def _in_proj_permutation():
    perm = np.arange(IN_WIDTH)
    ret = np.empty(RET_QK, np.int64)
    for n in range(RET_QK):
        half, h, i = n // LANES, (n % LANES) // 32, n % 32
        ret[n] = RET_DK * h + 32 * half + i
    perm[OFF_RQ:OFF_RQ + RET_QK] = OFF_RQ + ret
    perm[OFF_RK:OFF_RK + RET_QK] = OFF_RK + ret
    dif = np.empty(DIFF_QK, np.int64)
    for n in range(DIFF_QK):
        hp, r = n // 256, n % 256
        part, gl, i = r // LANES, (r % LANES) // 32, r % 32
        h, m = 2 * hp + gl // 2, gl % 2
        d = (16 * part + i) if i < 16 else (32 + 16 * part + (i - 16))
        dif[n] = 2 * DIFF_DK * h + DIFF_DK * m + d
    perm[OFF_DQ:OFF_DQ + DIFF_QK] = OFF_DQ + dif
    perm[OFF_DK:OFF_DK + DIFF_QK] = OFF_DK + dif
    return perm


def _rope_tables():
    def angles(pos, dim):
        inv = ROPE_BASE ** (-jnp.arange(0, dim, 2, dtype=F32) / dim)
        return pos.astype(F32)[:, None] * inv[None, :]

    t = jnp.arange(SEQ)
    ang_r = jnp.tile(angles(t, RET_DK), (1, RET_HEADS))
    half = DIFF_DK // 2
    ang_d = jnp.concatenate([angles(t // GRID_W, half), angles(t % GRID_W, half)], axis=1)
    ang_d = jnp.tile(ang_d, (1, 4))
    return jnp.cos(ang_r), jnp.sin(ang_r), jnp.cos(ang_d), jnp.sin(ang_d)


ADA_TN = 1536


def _ada_kernel(cc_ref, w_ref, b_ref, o_ref):
    s = _silu(cc_ref[...]).astype(BF16)
    o_ref[...] = jnp.dot(s, w_ref[...].astype(BF16), preferred_element_type=F32) + b_ref[...]


def _ada(cc, ada_w, ada_b):
    n = N_MOD * D_MODEL
    return pl.pallas_call(
        _ada_kernel,
        grid=(DEPTH, n // ADA_TN),
        in_specs=[
            pl.BlockSpec((MOD_ROWS, D_MODEL), lambda i, j: (0, 0)),
            pl.BlockSpec((None, D_MODEL, ADA_TN), lambda i, j: (i, 0, j)),
            pl.BlockSpec((None, 1, ADA_TN), lambda i, j: (i, 0, j)),
        ],
        out_specs=pl.BlockSpec((None, MOD_ROWS, ADA_TN), lambda i, j: (i, 0, j)),
        out_shape=jax.ShapeDtypeStruct((DEPTH, MOD_ROWS, n), F32),
        compiler_params=_params(2),
        name="ada_mod",
    )(cc, ada_w, ada_b.reshape(DEPTH, 1, n))


def _rmsnorm_mod(x, nw, shift, scale):
    ms = jnp.mean(x * x, axis=-1, keepdims=True)
    y = x * lax.rsqrt(ms + EPS) * nw
    return y * (1.0 + scale) + shift


def _inproj_kernel(*refs, rope):
    if rope:
        x_ref, mod_ref, nw_ref, w_ref, cr_ref, sr_ref, cd_ref, sd_ref, p_ref = refs
    else:
        x_ref, mod_ref, nw_ref, w_ref, p_ref = refs
    d = D_MODEL
    h = _rmsnorm_mod(x_ref[...], nw_ref[...], mod_ref[:, 0:d], mod_ref[:, d:2 * d]).astype(BF16)

    def proj(off, width):
        return jnp.dot(h, w_ref[:, off:off + width], preferred_element_type=F32)

    def rope_store(off, cos, sin, mult):
        z = proj(off, 2 * LANES)
        if rope:
            za, zb = z[:, :LANES], z[:, LANES:]
            p_ref[:, off:off + LANES] = ((za * cos - zb * sin) * mult).astype(BF16)
            p_ref[:, off + LANES:off + 2 * LANES] = ((zb * cos + za * sin) * mult).astype(BF16)
        else:
            p_ref[:, off:off + 2 * LANES] = (z * mult).astype(BF16)

    if rope:
        cr, sr, cd, sd = cr_ref[...], sr_ref[...], cd_ref[...], sd_ref[...]
    else:
        cr = sr = cd = sd = None
    rope_store(OFF_RQ, cr, sr, 1.0)
    rope_store(OFF_RK, cr, sr, RET_DK ** -0.5)
    for g in range(DIFF_QK // (2 * LANES)):
        rope_store(OFF_DQ + 2 * LANES * g, cd, sd, DIFF_DK ** -0.5 * LOG2_E)
        rope_store(OFF_DK + 2 * LANES * g, cd, sd, 1.0)
    for off, width in ((OFF_RV, 2 * RET_WIDTH), (OFF_DV, DIFF_WIDTH), (OFF_CV, 2 * CONV_CH)):
        p_ref[:, off:off + width] = proj(off, width).astype(BF16)


def _inproj(xs, mod, nw, w_bf, tables, tm):
    rows = xs.shape[1]
    rope = tables is not None
    const = lambda shape: pl.BlockSpec(shape, lambda t, b: (0,) * len(shape))
    mod_row = (lambda t, b: (b, 0, 0)) if rope else (lambda t, b: (BATCH, 0, 0))
    in_specs = [
        pl.BlockSpec((None, tm, D_MODEL), lambda t, b: (b, t, 0)),
        pl.BlockSpec((None, 1, N_MOD * D_MODEL), mod_row),
        const((1, D_MODEL)),
        const((D_MODEL, IN_WIDTH)),
    ]
    args = [xs, mod, nw, w_bf]
    if rope:
        in_specs += [pl.BlockSpec((tm, LANES), lambda t, b: (t, 0))] * 4
        args += list(tables)
    return pl.pallas_call(
        functools.partial(_inproj_kernel, rope=rope),
        grid=(rows // tm, BATCH),
        in_specs=in_specs,
        out_specs=pl.BlockSpec((None, tm, IN_WIDTH), lambda t, b: (b, t, 0)),
        out_shape=jax.ShapeDtypeStruct((BATCH, rows, IN_WIDTH), BF16),
        compiler_params=_params(2),
        name="in_proj" if rope else "in_proj_ctx",
    )(*args)


RET_C = 256
RET_LAT_NC = SEQ // RET_C
RET_NC = RET_LAT_NC + CTX_LEN // RET_C
QD_F, QD_B, KD_F, KD_B, CD_F, CD_B = range(6)


def _ret_kernel(*refs, need_ctx):
    ld_ref, ql, kl, vl, gl, qc_, kc_, vc_, gc_, gnw_ref = refs[:10]
    if need_ctx:
        ol, oc, dtab, dvec, kvbuf, sbuf = refs[10:]
    else:
        ol, dtab, dvec, kvbuf, sbuf = refs[10:]
        oc = None
    c = RET_C
    w = RET_QK
    assert c == w

    def chunk(n):
        if n < RET_LAT_NC:
            return ql, kl, vl, gl, ol, slice(n * c, (n + 1) * c)
        return qc_, kc_, vc_, gc_, oc, slice(0, c)

    lane = lax.broadcasted_iota(jnp.int32, (c, w), 1)
    rowi = lax.broadcasted_iota(jnp.int32, (c, w), 0)
    row = rowi.astype(F32)
    hq = (lane % LANES) // 32
    hv = lane // RET_DV
    rel = rowi - lane
    rel_f = jnp.maximum(rel, 0).astype(F32)
    rel_b = jnp.maximum(-rel, 0).astype(F32)

    lgq = [jnp.zeros((c, w), F32), jnp.zeros((c, w), F32)]
    lgv = [jnp.zeros((c, w), F32), jnp.zeros((c, w), F32)]
    for h in range(RET_HEADS):
        lg = [-jnp.exp(jnp.full((c, w), ld_ref[d, h], F32)) for d in range(2)]
        dtab[h * c:(h + 1) * c, :] = jnp.where(
            rel > 0, jnp.exp(lg[0] * rel_f), jnp.where(rel < 0, jnp.exp(lg[1] * rel_b), 2.0))
        for d in range(2):
            lgq[d] = jnp.where(hq == h, lg[d], lgq[d])
            lgv[d] = jnp.where(hv == h, lg[d], lgv[d])
    dvec[QD_F] = jnp.exp(lgq[0] * (row + 1.0))
    dvec[QD_B] = jnp.exp(lgq[1] * (c - row))
    dvec[KD_F] = jnp.exp(lgq[0] * (c - 1.0 - row))
    dvec[KD_B] = jnp.exp(lgq[1] * row)
    dvec[CD_F] = jnp.exp(lgv[0] * float(c))
    dvec[CD_B] = jnp.exp(lgv[1] * float(c))

    same_head = ((rowi % LANES) // 32) == (lane // RET_DV)
    group_avg = jnp.where((rowi // RET_DV) == (lane // RET_DV), 1.0 / RET_DV, 0.0).astype(BF16)

    def hi_lo_dot(a, m):
        hi = a.astype(BF16)
        lo = (a - hi.astype(F32)).astype(BF16)
        return (jnp.dot(hi, m, preferred_element_type=F32)
                + jnp.dot(lo, m, preferred_element_type=F32))

    for n in range(RET_NC):
        _, k_ref, v_ref, _, _, rows = chunk(n)
        kc = k_ref[rows, :].astype(F32)
        k2 = jnp.concatenate([(kc * dvec[KD_F]).astype(BF16), (kc * dvec[KD_B]).astype(BF16)], axis=1)
        kv = lax.dot_general(k2, v_ref[rows, :], (((0,), (0,)), ((), ())),
                             preferred_element_type=F32)
        kvbuf[0, n] = jnp.where(same_head, kv[:w], 0.0)
        kvbuf[1, n] = jnp.where(same_head, kv[w:], 0.0)

    orders = ([RET_LAT_NC] + list(range(RET_LAT_NC)), [RET_LAT_NC] + list(range(RET_LAT_NC - 1, -1, -1)))
    for d, order in enumerate(orders):
        s = jnp.zeros((w, w), F32)
        for n in order[:-1]:
            sbuf[d, n] = s.astype(BF16)
            s = s * dvec[CD_F + d] + kvbuf[d, n]
        sbuf[d, order[-1]] = s.astype(BF16)

    for n in range(RET_NC if need_ctx else RET_LAT_NC):
        q_ref, k_ref, v_ref, g_ref, o_ref, rows = chunk(n)
        qc, kc, vc = q_ref[rows, :], k_ref[rows, :], v_ref[rows, :]
        q4 = jnp.concatenate([jnp.where(hq == h, qc, jnp.zeros_like(qc)) for h in range(RET_HEADS)],
                             axis=0)
        s = lax.dot_general(q4, kc, (((1,), (1,)), ((), ())), preferred_element_type=F32)
        pw = (s * dtab[...]).astype(BF16)
        qf = qc.astype(F32)
        lhs = jnp.concatenate([pw[h * c:(h + 1) * c] for h in range(RET_HEADS)]
                              + [(qf * dvec[QD_F]).astype(BF16), (qf * dvec[QD_B]).astype(BF16)], axis=1)
        rhs = jnp.concatenate([jnp.where(hv == h, vc, jnp.zeros_like(vc)) for h in range(RET_HEADS)]
                              + [sbuf[0, n], sbuf[1, n]], axis=0)
        o = jnp.dot(lhs, rhs, preferred_element_type=F32)
        mu = hi_lo_dot(o, group_avg)
        dev = o - mu
        var = hi_lo_dot(dev * dev, group_avg)
        t = dev * lax.rsqrt(var + EPS) * gnw_ref[...]
        g = g_ref[rows, :].astype(F32)
        o_ref[rows, :] = (t * _silu(g)).astype(BF16)


def _retention(p_lat, p_ctx, ret_log_decay, gn_w, need_ctx):
    w = RET_QK
    lat = lambda j: pl.BlockSpec((None, SEQ, w), lambda b: (b, 0, j))
    ctx = lambda j: pl.BlockSpec((None, CTX_LEN, w), lambda b: (b, 0, j))
    cols = (OFF_RQ // w, OFF_RK // w, OFF_RV // w, OFF_RG // w)
    out_specs = [lat(0)]
    out_shape = [jax.ShapeDtypeStruct((BATCH, SEQ, w), BF16)]
    if need_ctx:
        out_specs.append(ctx(0))
        out_shape.append(jax.ShapeDtypeStruct((BATCH, CTX_LEN, w), BF16))
    outs = pl.pallas_call(
        functools.partial(_ret_kernel, need_ctx=need_ctx),
        grid=(BATCH,),
        in_specs=[pl.BlockSpec(memory_space=pltpu.SMEM)]
                 + [lat(j) for j in cols] + [ctx(j) for j in cols]
                 + [pl.BlockSpec((1, w), lambda b: (0, 0))],
        out_specs=out_specs,
        out_shape=out_shape,
        scratch_shapes=[
            pltpu.VMEM((RET_HEADS * RET_C, RET_C), F32),
            pltpu.VMEM((6, RET_C, w), F32),
            pltpu.VMEM((2, RET_NC, w, w), F32),
            pltpu.VMEM((2, RET_NC, w, w), BF16),
        ],
        compiler_params=_params(1),
        name="retention",
    )(ret_log_decay, p_lat, p_lat, p_lat, p_lat, p_ctx, p_ctx, p_ctx, p_ctx, gn_w)
    return (outs[0], outs[1]) if need_ctx else (outs[0], None)


DIFF_QT = 512
DIFF_SUB = 256
DIFF_RB = 16


def _diff_kernel(*refs, lam_init, n_sub, n_kv):
    q_ref = refs[0]
    k_refs = refs[1:1 + n_kv]
    v_refs = refs[1 + n_kv:1 + 2 * n_kv]
    lam_ref, gnw_ref, o_ref, sbuf, ebuf = refs[1 + 2 * n_kv:]
    sub = DIFF_SUB
    seg = [0]
    for k_ref in k_refs:
        seg.append(seg[-1] + k_ref.shape[0])
    lp = lam_ref[...]
    lam = (jnp.exp(jnp.sum(lp[0:1] * lp[1:2], axis=-1, keepdims=True))
           - jnp.exp(jnp.sum(lp[2:3] * lp[3:4], axis=-1, keepdims=True)) + lam_init)
    lane = lax.broadcasted_iota(jnp.int32, (sub, 2 * LANES), 1)
    grp = (lane % LANES) // 32
    items = [(qs, h) for qs in range(n_sub) for h in range(DIFF_HEADS)]

    def scores(idx, slot):
        qs, h = items[idx]
        hp, hl = divmod(h, 2)
        cols = slice(2 * LANES * hp, 2 * LANES * (hp + 1))
        qh = q_ref[qs * sub:(qs + 1) * sub, cols]
        q2 = jnp.concatenate([jnp.where(grp == 2 * hl + m, qh, jnp.zeros_like(qh))
                              for m in range(2)], axis=0)
        mx = None
        for j, k_ref in enumerate(k_refs):
            s = lax.dot_general(q2, k_ref[:, cols], (((1,), (1,)), ((), ())),
                                preferred_element_type=F32)
            sbuf[slot, :, seg[j]:seg[j + 1]] = s
            mj = jnp.max(s, axis=-1, keepdims=True)
            mx = mj if mx is None else jnp.maximum(mx, mj)
        return mx

    def softmax(slot, mx):
        sums = []
        for r0 in range(0, 2 * sub, DIFF_RB):
            e = jnp.exp2(sbuf[slot, r0:r0 + DIFF_RB, :] - mx[r0:r0 + DIFF_RB])
            sums.append(jnp.sum(e, axis=-1, keepdims=True))
            ebuf[slot, r0:r0 + DIFF_RB, :] = e.astype(BF16)
        return (jnp.concatenate(sums[:sub // DIFF_RB], axis=0),
                jnp.concatenate(sums[sub // DIFF_RB:], axis=0))

    def output(idx, slot, l0, l1):
        qs, h = items[idx]
        ratio = (lam * l0 / l1).astype(BF16)
        a = ebuf[slot, 0:sub] - ratio * ebuf[slot, sub:2 * sub]
        hcols = slice(DIFF_DV * h, DIFF_DV * (h + 1))
        o = None
        for j, v_ref in enumerate(v_refs):
            oj = jnp.dot(a[:, seg[j]:seg[j + 1]], v_ref[:, hcols], preferred_element_type=F32)
            o = oj if o is None else o + oj
        o = o / l0
        ms = jnp.mean(o * o, axis=-1, keepdims=True)
        y = o * lax.rsqrt(ms + EPS) * gnw_ref[:, hcols]
        o_ref[qs * sub:(qs + 1) * sub, hcols] = (y * (1.0 - lam_init)).astype(BF16)

    n = len(items)
    mxs = {0: scores(0, 0)}
    if n > 1:
        mxs[1] = scores(1, 1)
    ls = {0: softmax(0, mxs.pop(0))}
    for idx in range(n):
        if idx + 2 < n:
            mxs[idx + 2] = scores(idx + 2, idx % 2)
        if idx + 1 < n:
            ls[idx + 1] = softmax((idx + 1) % 2, mxs.pop(idx + 1))
        output(idx, idx % 2, *ls.pop(idx))


def _diff_attention(p_q, p_kv, diff_lam, gn_w, lam_init, qt):
    w = DIFF_QK
    rows = p_q.shape[1]
    n_kv = len(p_kv)
    keys = sum(a.shape[1] for a in p_kv)
    kv_spec = lambda a, j: pl.BlockSpec((None, a.shape[1], w), lambda b, t: (b, 0, j))
    return pl.pallas_call(
        functools.partial(_diff_kernel, lam_init=lam_init, n_sub=qt // DIFF_SUB, n_kv=n_kv),
        grid=(BATCH, rows // qt),
        in_specs=[pl.BlockSpec((None, qt, w), lambda b, t: (b, t, OFF_DQ // w))]
                 + [kv_spec(a, OFF_DK // w) for a in p_kv] + [kv_spec(a, OFF_DV // w) for a in p_kv]
                 + [pl.BlockSpec((4, DIFF_DK), lambda b, t: (0, 0)), pl.BlockSpec((1, w), lambda b, t: (0, 0))],
        out_specs=pl.BlockSpec((None, qt, w), lambda b, t: (b, t, 0)),
        out_shape=jax.ShapeDtypeStruct((BATCH, rows, w), BF16),
        scratch_shapes=[pltpu.VMEM((2, 2 * DIFF_SUB, keys), F32),
                        pltpu.VMEM((2, 2 * DIFF_SUB, keys), BF16)],
        compiler_params=_params(2),
        name="diff_attn" if n_kv > 1 else "diff_attn_ctx",
    )(p_q, *p_kv, *p_kv, diff_lam, gn_w)


CONV_GAP = 2 * SUBLANES
CONV_TILE = 128
CONV_WIN = CONV_TILE + 2 * SUBLANES
CONV_KW = 256


def _conv_rows(n):
    return CONV_GAP + n + 2 * CONV_GAP


def _conv_kernel(*refs, n_seq):
    ins = [refs[2 * i:2 * i + 2] for i in range(n_seq)]
    w_ref, b_ref, lnw_ref, lnb_ref = refs[2 * n_seq:2 * n_seq + 4]
    outs = refs[2 * n_seq + 4:3 * n_seq + 4]
    ubufs = refs[3 * n_seq + 4:4 * n_seq + 4]
    zbuf = refs[4 * n_seq + 4]
    ch = CONV_CH
    zbuf[...] = jnp.zeros_like(zbuf)

    kcol = lax.broadcasted_iota(jnp.int32, (CONV_TILE, SUBLANES * CONV_KW), 1)
    krow = lax.broadcasted_iota(jnp.int32, (CONV_TILE, SUBLANES * CONV_KW), 0)
    sel = jnp.where((kcol % CONV_KW) == krow + kcol // CONV_KW, 1.0, 0.0).astype(BF16)

    for (cv_ref, cg_ref), o_ref, ubuf in zip(ins, outs, ubufs):
        n = cv_ref.shape[0]
        ubuf[0:CONV_GAP, :] = jnp.zeros((CONV_GAP, ch), F32)
        ubuf[CONV_GAP + n:_conv_rows(n), :] = jnp.zeros((2 * CONV_GAP, ch), F32)
        ubuf[CONV_GAP:CONV_GAP + n, :] = cv_ref[...].astype(F32) * _sigmoid(cg_ref[...].astype(F32))
        for r0 in range(0, n, CONV_TILE):
            t0 = CONV_GAP + r0
            for r in range(SUBLANES):
                z = None
                for a in range(-2, 2):
                    o = SUBLANES * a + r
                    if abs(o) > CONV_PAD:
                        continue
                    lo = t0 + SUBLANES * a
                    term = ubuf[lo:lo + CONV_WIN, :] * w_ref[o + CONV_PAD:o + CONV_PAD + 1, :]
                    z = term if z is None else z + term
                zbuf[CONV_KW * r:CONV_KW * r + CONV_WIN, :] = z.astype(BF16)
            y = jnp.dot(sel, zbuf[...], preferred_element_type=F32) + b_ref[...]
            mu = jnp.mean(y, axis=-1, keepdims=True)
            dev = y - mu
            var = jnp.mean(dev * dev, axis=-1, keepdims=True)
            zn = dev * lax.rsqrt(var + EPS) * lnw_ref[...] + lnb_ref[...]
            o_ref[r0:r0 + CONV_TILE, :] = _silu(zn).astype(BF16)


def _conformer_conv(ps, conv_w, conv_b, ln_w, ln_b):
    ch = CONV_CH
    vec = pl.BlockSpec((1, ch), lambda b: (0, 0))
    in_specs, args = [], []
    for p in ps:
        n = p.shape[1]
        in_specs += [pl.BlockSpec((None, n, ch), lambda b: (b, 0, OFF_CV // ch)),
                     pl.BlockSpec((None, n, ch), lambda b: (b, 0, OFF_CG // ch))]
        args += [p, p]
    outs = pl.pallas_call(
        functools.partial(_conv_kernel, n_seq=len(ps)),
        grid=(BATCH,),
        in_specs=in_specs + [pl.BlockSpec((CONV_WIDTH, ch), lambda b: (0, 0)), vec, vec, vec],
        out_specs=[pl.BlockSpec((None, p.shape[1], ch), lambda b: (b, 0, 0)) for p in ps],
        out_shape=[jax.ShapeDtypeStruct((BATCH, p.shape[1], ch), BF16) for p in ps],
        scratch_shapes=[pltpu.VMEM((_conv_rows(p.shape[1]), ch), F32) for p in ps]
                       + [pltpu.VMEM((SUBLANES * CONV_KW, ch), BF16)],
        compiler_params=_params(1),
        name="conformer_conv",
    )(*args, conv_w, conv_b, ln_w, ln_b)
    return list(outs)


def _out_mlp_kernel(ret_ref, dif_ref, cnv_ref, x_ref, mod_ref, nw_ref, wo_ref, w1_ref, w2_ref,
                    fin_ref, o_ref, *, final):
    d = D_MODEL
    y = jnp.dot(ret_ref[...], wo_ref[0:RET_WIDTH, :], preferred_element_type=F32)
    y = y + jnp.dot(dif_ref[...], wo_ref[RET_WIDTH:RET_WIDTH + DIFF_WIDTH, :],
                    preferred_element_type=F32)
    y = y + jnp.dot(cnv_ref[...], wo_ref[RET_WIDTH + DIFF_WIDTH:MIX_WIDTH, :],
                    preferred_element_type=F32)
    x1 = x_ref[...] + mod_ref[:, 2 * d:3 * d] * y
    h2 = _rmsnorm_mod(x1, nw_ref[...], mod_ref[:, 3 * d:4 * d], mod_ref[:, 4 * d:5 * d]).astype(BF16)
    u = jnp.maximum(jnp.dot(h2, w1_ref[...], preferred_element_type=F32), 0.0)
    z = jnp.dot((u * u).astype(BF16), w2_ref[...], preferred_element_type=F32)
    x2 = x1 + mod_ref[:, 5 * d:6 * d] * z
    if final:
        ms = jnp.mean(x2 * x2, axis=-1, keepdims=True)
        x2 = x2 * lax.rsqrt(ms + EPS) * fin_ref[...]
    o_ref[...] = x2


def _out_mlp(ret, dif, cnv, xs, mod, nw, wo, w1, w2, fin, tm, is_ctx, final):
    rows = xs.shape[1]
    const = lambda shape: pl.BlockSpec(shape, lambda b, t: (0,) * len(shape),
                                       pipeline_mode=pl.Buffered(1))
    tile = lambda width: pl.BlockSpec((None, tm, width), lambda b, t: (b, t, 0))
    mod_row = (lambda b, t: (BATCH, 0, 0)) if is_ctx else (lambda b, t: (b, 0, 0))
    return pl.pallas_call(
        functools.partial(_out_mlp_kernel, final=final),
        grid=(BATCH, rows // tm),
        in_specs=[
            tile(RET_WIDTH), tile(DIFF_WIDTH), tile(CONV_CH), tile(D_MODEL),
            pl.BlockSpec((None, 1, N_MOD * D_MODEL), mod_row),
            const((1, D_MODEL)),
            const((MIX_WIDTH, D_MODEL)), const((D_MODEL, D_FF)), const((D_FF, D_MODEL)),
            const((1, D_MODEL)),
        ],
        out_specs=tile(D_MODEL),
        out_shape=jax.ShapeDtypeStruct((BATCH, rows, D_MODEL), F32),
        compiler_params=_params(2),
        name="out_mlp_ctx" if is_ctx else "out_mlp",
    )(ret, dif, cnv, xs, mod, nw, wo, w1, w2, fin)


def kernel(x, c, ctx, c_ctx, norm1_w, norm2_w, ada_w, ada_b, w_in, ret_log_decay, ret_gn_w, diff_lam, diff_gn_w, conv_w, conv_b, conv_ln_w, conv_ln_b, w_out, mlp_w1, mlp_w2, final_norm_w):
    assert x.shape == (BATCH, SEQ, D_MODEL) and ctx.shape == (BATCH, CTX_LEN, D_MODEL)
    perm = _in_proj_permutation()
    tables = _rope_tables()
    cc = jnp.concatenate([c, c_ctx[None, :], jnp.zeros((MOD_ROWS - BATCH - 1, D_MODEL), F32)], axis=0)
    mod_all = _ada(cc, ada_w, ada_b).reshape(DEPTH, MOD_ROWS, 1, N_MOD * D_MODEL)
    fin = final_norm_w.reshape(1, D_MODEL)
    for i in range(DEPTH):
        last = i == DEPTH - 1
        lam_init = 0.8 - 0.6 * math.exp(-0.3 * i)
        mod = mod_all[i]
        nw1 = norm1_w[i].reshape(1, D_MODEL)
        nw2 = norm2_w[i].reshape(1, D_MODEL)
        w_bf = w_in[i][:, perm].astype(BF16)
        wo, w1, w2 = w_out[i].astype(BF16), mlp_w1[i].astype(BF16), mlp_w2[i].astype(BF16)
        dgn = diff_gn_w[i].reshape(1, DIFF_WIDTH)
        p_lat = _inproj(x, mod, nw1, w_bf, tables, TM_LAT)
        p_ctx = _inproj(ctx, mod, nw1, w_bf, None, TM_CTX)
        ret_lat, ret_ctx = _retention(p_lat, p_ctx, ret_log_decay[i], ret_gn_w[i].reshape(1, RET_WIDTH),
                                      not last)
        dif_lat = _diff_attention(p_lat, [p_lat, p_ctx], diff_lam[i], dgn, lam_init, DIFF_QT)
        conv_args = (conv_w[i], conv_b[i].reshape(1, CONV_CH),
                     conv_ln_w[i].reshape(1, CONV_CH), conv_ln_b[i].reshape(1, CONV_CH))
        if last:
            cnv_lat, = _conformer_conv([p_lat], *conv_args)
        else:
            cnv_lat, cnv_ctx = _conformer_conv([p_lat, p_ctx], *conv_args)
            dif_ctx = _diff_attention(p_ctx, [p_ctx], diff_lam[i], dgn, lam_init, CTX_LEN)
            ctx = _out_mlp(ret_ctx, dif_ctx, cnv_ctx, ctx, mod, nw2, wo, w1, w2, fin, TM_CTX, True, False)
        x = _out_mlp(ret_lat, dif_lat, cnv_lat, x, mod, nw2, wo, w1, w2, fin, TM_LAT, False, last)
    return x
```

```python
import functools
import math

import numpy as np
import jax
import jax.numpy as jnp
from jax import lax
from jax.experimental import pallas as pl
from jax.experimental.pallas import tpu as pltpu

D_MODEL = 1024
BATCH = 16
SEQ = 2048
DEPTH = 2
CTX_LEN = 256
GRID_W = 64
EPS = 1e-6
ROPE_BASE = 10000.0
RET_HEADS = 4
RET_DK = 64
RET_DV = 64
RET_QK = RET_HEADS * RET_DK
RET_WIDTH = RET_HEADS * RET_DV
DIFF_HEADS = 4
DIFF_DK = 64
DIFF_DV = 2 * DIFF_DK
DIFF_QK = DIFF_HEADS * 2 * DIFF_DK
DIFF_WIDTH = DIFF_HEADS * DIFF_DV
CONV_CH = D_MODEL // 4
CONV_WIDTH = 31
CONV_PAD = (CONV_WIDTH - 1) // 2
MIX_WIDTH = RET_WIDTH + DIFF_WIDTH + CONV_CH
IN_WIDTH = 2 * RET_QK + 2 * RET_WIDTH + 2 * DIFF_QK + DIFF_WIDTH + 2 * CONV_CH
D_FF = 4 * D_MODEL
N_MOD = 6

TOK = SEQ + CTX_LEN
LANES = 128
SUBLANES = 8
TM_LAT = 512
TM_CTX = CTX_LEN
MOD_ROWS = 24
VMEM_LIMIT = 56 * 1024 * 1024

OFF_RQ = 0
OFF_RK = OFF_RQ + RET_QK
OFF_RV = OFF_RK + RET_QK
OFF_RG = OFF_RV + RET_WIDTH
OFF_DQ = OFF_RG + RET_WIDTH
OFF_DK = OFF_DQ + DIFF_QK
OFF_CV = OFF_DK + DIFF_QK
OFF_CG = OFF_CV + CONV_CH
P_WIDTH = OFF_CG + CONV_CH
OFF_DV = P_WIDTH
SRC_DV = OFF_DK + DIFF_QK
SRC_CV = SRC_DV + DIFF_WIDTH

F32 = jnp.float32
BF16 = jnp.bfloat16
LOG2_E = math.log2(math.e)


def _params(n_axes):
    return pltpu.CompilerParams(dimension_semantics=("arbitrary",) * n_axes,
                                vmem_limit_bytes=VMEM_LIMIT)


def _sigmoid(x):
    return 1.0 / (1.0 + jnp.exp(-x))


def _silu(x):
    return x * _sigmoid(x)


def _in_proj_permutation():
    perm = np.arange(IN_WIDTH)
    perm[OFF_CV:OFF_CV + 2 * CONV_CH] = SRC_CV + np.arange(2 * CONV_CH)
    perm[OFF_DV:OFF_DV + DIFF_WIDTH] = SRC_DV + np.arange(DIFF_WIDTH)
    ret = np.empty(RET_QK, np.int64)
    for n in range(RET_QK):
        half, h, i = n // LANES, (n % LANES) // 32, n % 32
        ret[n] = RET_DK * h + 32 * half + i
    perm[OFF_RQ:OFF_RQ + RET_QK] = OFF_RQ + ret
    perm[OFF_RK:OFF_RK + RET_QK] = OFF_RK + ret
    dif = np.empty(DIFF_QK, np.int64)
    for n in range(DIFF_QK):
        hp, r = n // 256, n % 256
        part, gl, i = r // LANES, (r % LANES) // 32, r % 32
        h, m = 2 * hp + gl // 2, gl % 2
        d = (16 * part + i) if i < 16 else (32 + 16 * part + (i - 16))
        dif[n] = 2 * DIFF_DK * h + DIFF_DK * m + d
    perm[OFF_DQ:OFF_DQ + DIFF_QK] = OFF_DQ + dif
    perm[OFF_DK:OFF_DK + DIFF_QK] = OFF_DK + dif
    return perm


def _rope_tables():
    def angles(pos, dim):
        inv = ROPE_BASE ** (-jnp.arange(0, dim, 2, dtype=F32) / dim)
        return pos.astype(F32)[:, None] * inv[None, :]

    t = jnp.arange(SEQ)
    ang_r = jnp.tile(angles(t, RET_DK), (1, RET_HEADS))
    half = DIFF_DK // 2
    ang_d = jnp.concatenate([angles(t // GRID_W, half), angles(t % GRID_W, half)], axis=1)
    ang_d = jnp.tile(ang_d, (1, 4))
    return jnp.cos(ang_r), jnp.sin(ang_r), jnp.cos(ang_d), jnp.sin(ang_d)


ADA_TN = 1536


def _ada_kernel(cc_ref, w_ref, b_ref, o_ref):
    s = _silu(cc_ref[...]).astype(BF16)
    o_ref[...] = jnp.dot(s, w_ref[...].astype(BF16), preferred_element_type=F32) + b_ref[...]


def _ada(cc, ada_w, ada_b):
    n = N_MOD * D_MODEL
    return pl.pallas_call(
        _ada_kernel,
        grid=(DEPTH, n // ADA_TN),
        in_specs=[
            pl.BlockSpec((MOD_ROWS, D_MODEL), lambda i, j: (0, 0)),
            pl.BlockSpec((None, D_MODEL, ADA_TN), lambda i, j: (i, 0, j)),
            pl.BlockSpec((None, 1, ADA_TN), lambda i, j: (i, 0, j)),
        ],
        out_specs=pl.BlockSpec((None, MOD_ROWS, ADA_TN), lambda i, j: (i, 0, j)),
        out_shape=jax.ShapeDtypeStruct((DEPTH, MOD_ROWS, n), F32),
        compiler_params=_params(2),
        name="ada_mod",
    )(cc, ada_w, ada_b.reshape(DEPTH, 1, n))


def _rmsnorm_mod(x, nw, shift, scale):
    ms = jnp.mean(x * x, axis=-1, keepdims=True)
    y = x * lax.rsqrt(ms + EPS) * nw
    return y * (1.0 + scale) + shift


def _inproj_kernel(*refs, rope):
    if rope:
        x_ref, mod_ref, nw_ref, w_ref, cr_ref, sr_ref, cd_ref, sd_ref, p_ref, vt_ref = refs
    else:
        x_ref, mod_ref, nw_ref, w_ref, p_ref, vt_ref = refs
    d = D_MODEL
    h = _rmsnorm_mod(x_ref[...], nw_ref[...], mod_ref[:, 0:d], mod_ref[:, d:2 * d]).astype(BF16)

    def proj(off, width):
        return jnp.dot(h, w_ref[:, off:off + width], preferred_element_type=F32)

    def rope_store(off, cos, sin, mult):
        z = proj(off, 2 * LANES)
        if rope:
            za, zb = z[:, :LANES], z[:, LANES:]
            p_ref[:, off:off + LANES] = ((za * cos - zb * sin) * mult).astype(BF16)
            p_ref[:, off + LANES:off + 2 * LANES] = ((zb * cos + za * sin) * mult).astype(BF16)
        else:
            p_ref[:, off:off + 2 * LANES] = (z * mult).astype(BF16)

    if rope:
        cr, sr, cd, sd = cr_ref[...], sr_ref[...], cd_ref[...], sd_ref[...]
    else:
        cr = sr = cd = sd = None
    rope_store(OFF_RQ, cr, sr, 1.0)
    rope_store(OFF_RK, cr, sr, RET_DK ** -0.5)
    for g in range(DIFF_QK // (2 * LANES)):
        rope_store(OFF_DQ + 2 * LANES * g, cd, sd, DIFF_DK ** -0.5 * LOG2_E)
        rope_store(OFF_DK + 2 * LANES * g, cd, sd, 1.0)
    for off, width in ((OFF_RV, 2 * RET_WIDTH), (OFF_CV, 2 * CONV_CH)):
        p_ref[:, off:off + width] = proj(off, width).astype(BF16)
    vt_ref[...] = proj(OFF_DV, DIFF_WIDTH).T.astype(BF16)


def _inproj(xs, mod, nw, w_bf, layer, tables, tm):
    rows = xs.shape[1]
    rope = tables is not None
    const = lambda shape: pl.BlockSpec(shape, lambda t, b: (0,) * len(shape))
    mod_row = (lambda t, b: (b, 0, 0)) if rope else (lambda t, b: (BATCH, 0, 0))
    in_specs = [
        pl.BlockSpec((None, tm, D_MODEL), lambda t, b: (b, t, 0)),
        pl.BlockSpec((None, 1, N_MOD * D_MODEL), mod_row),
        const((1, D_MODEL)),
        pl.BlockSpec((None, D_MODEL, IN_WIDTH), lambda t, b: (layer, 0, 0)),
    ]
    args = [xs, mod, nw, w_bf]
    if rope:
        in_specs += [pl.BlockSpec((tm, LANES), lambda t, b: (t, 0))] * 4
        args += list(tables)
    return pl.pallas_call(
        functools.partial(_inproj_kernel, rope=rope),
        grid=(rows // tm, BATCH),
        in_specs=in_specs,
        out_specs=[pl.BlockSpec((None, tm, P_WIDTH), lambda t, b: (b, t, 0)),
                   pl.BlockSpec((None, DIFF_WIDTH, tm), lambda t, b: (b, 0, t))],
        out_shape=[jax.ShapeDtypeStruct((BATCH, rows, P_WIDTH), BF16),
                   jax.ShapeDtypeStruct((BATCH, DIFF_WIDTH, rows), BF16)],
        compiler_params=_params(2),
        name="in_proj" if rope else "in_proj_ctx",
    )(*args)


RET_C = 256
RET_LAT_NC = SEQ // RET_C
RET_NC = RET_LAT_NC + CTX_LEN // RET_C
QD_F, QD_B, KD_F, KD_B, CD_F, CD_B = range(6)


def _ret_kernel(*refs, need_ctx):
    ld_ref, ql, kl, vl, gl, qc_, kc_, vc_, gc_, gnw_ref = refs[:10]
    if need_ctx:
        ol, oc, dtab, dvec, kvbuf, sbuf = refs[10:]
    else:
        ol, dtab, dvec, kvbuf, sbuf = refs[10:]
        oc = None
    c = RET_C
    w = RET_QK
    assert c == w

    def chunk(n):
        if n < RET_LAT_NC:
            return ql, kl, vl, gl, ol, slice(n * c, (n + 1) * c)
        return qc_, kc_, vc_, gc_, oc, slice(0, c)

    lane = lax.broadcasted_iota(jnp.int32, (c, w), 1)
    rowi = lax.broadcasted_iota(jnp.int32, (c, w), 0)
    row = rowi.astype(F32)
    hq = (lane % LANES) // 32
    hv = lane // RET_DV
    rel = rowi - lane
    rel_f = jnp.maximum(rel, 0).astype(F32)
    rel_b = jnp.maximum(-rel, 0).astype(F32)

    lgq = [jnp.zeros((c, w), F32), jnp.zeros((c, w), F32)]
    lgv = [jnp.zeros((c, w), F32), jnp.zeros((c, w), F32)]
    for h in range(RET_HEADS):
        lg = [-jnp.exp(jnp.full((c, w), ld_ref[d, h], F32)) for d in range(2)]
        dtab[h * c:(h + 1) * c, :] = jnp.where(
            rel > 0, jnp.exp(lg[0] * rel_f), jnp.where(rel < 0, jnp.exp(lg[1] * rel_b), 2.0))
        for d in range(2):
            lgq[d] = jnp.where(hq == h, lg[d], lgq[d])
            lgv[d] = jnp.where(hv == h, lg[d], lgv[d])
    dvec[QD_F] = jnp.exp(lgq[0] * (row + 1.0))
    dvec[QD_B] = jnp.exp(lgq[1] * (c - row))
    dvec[KD_F] = jnp.exp(lgq[0] * (c - 1.0 - row))
    dvec[KD_B] = jnp.exp(lgq[1] * row)
    dvec[CD_F] = jnp.exp(lgv[0] * float(c))
    dvec[CD_B] = jnp.exp(lgv[1] * float(c))

    same_head = ((rowi % LANES) // 32) == (lane // RET_DV)
    group_avg = jnp.where((rowi // RET_DV) == (lane // RET_DV), 1.0 / RET_DV, 0.0).astype(BF16)

    def hi_lo_dot(a, m):
        hi = a.astype(BF16)
        lo = (a - hi.astype(F32)).astype(BF16)
        return (jnp.dot(hi, m, preferred_element_type=F32)
                + jnp.dot(lo, m, preferred_element_type=F32))

    for n in range(RET_NC):
        _, k_ref, v_ref, _, _, rows = chunk(n)
        kc = k_ref[rows, :].astype(F32)
        k2 = jnp.concatenate([(kc * dvec[KD_F]).astype(BF16), (kc * dvec[KD_B]).astype(BF16)], axis=1)
        kv = lax.dot_general(k2, v_ref[rows, :], (((0,), (0,)), ((), ())),
                             preferred_element_type=F32)
        kvbuf[0, n] = jnp.where(same_head, kv[:w], 0.0)
        kvbuf[1, n] = jnp.where(same_head, kv[w:], 0.0)

    orders = ([RET_LAT_NC] + list(range(RET_LAT_NC)), [RET_LAT_NC] + list(range(RET_LAT_NC - 1, -1, -1)))
    for d, order in enumerate(orders):
        s = jnp.zeros((w, w), F32)
        for n in order[:-1]:
            sbuf[d, n] = s.astype(BF16)
            s = s * dvec[CD_F + d] + kvbuf[d, n]
        sbuf[d, order[-1]] = s.astype(BF16)

    for n in range(RET_NC if need_ctx else RET_LAT_NC):
        q_ref, k_ref, v_ref, g_ref, o_ref, rows = chunk(n)
        qc, kc, vc = q_ref[rows, :], k_ref[rows, :], v_ref[rows, :]
        q4 = jnp.concatenate([jnp.where(hq == h, qc, jnp.zeros_like(qc)) for h in range(RET_HEADS)],
                             axis=0)
        s = lax.dot_general(q4, kc, (((1,), (1,)), ((), ())), preferred_element_type=F32)
        pw = (s * dtab[...]).astype(BF16)
        qf = qc.astype(F32)
        lhs = jnp.concatenate([pw[h * c:(h + 1) * c] for h in range(RET_HEADS)]
                              + [(qf * dvec[QD_F]).astype(BF16), (qf * dvec[QD_B]).astype(BF16)], axis=1)
        rhs = jnp.concatenate([jnp.where(hv == h, vc, jnp.zeros_like(vc)) for h in range(RET_HEADS)]
                              + [sbuf[0, n], sbuf[1, n]], axis=0)
        o = jnp.dot(lhs, rhs, preferred_element_type=F32)
        mu = hi_lo_dot(o, group_avg)
        dev = o - mu
        var = hi_lo_dot(dev * dev, group_avg)
        t = dev * lax.rsqrt(var + EPS) * gnw_ref[...]
        g = g_ref[rows, :].astype(F32)
        o_ref[rows, :] = (t * _silu(g)).astype(BF16)


def _retention(p_lat, p_ctx, ret_log_decay, gn_w, need_ctx):
    w = RET_QK
    lat = lambda j: pl.BlockSpec((None, SEQ, w), lambda b: (b, 0, j))
    ctx = lambda j: pl.BlockSpec((None, CTX_LEN, w), lambda b: (b, 0, j))
    cols = (OFF_RQ // w, OFF_RK // w, OFF_RV // w, OFF_RG // w)
    out_specs = [lat(0)]
    out_shape = [jax.ShapeDtypeStruct((BATCH, SEQ, w), BF16)]
    if need_ctx:
        out_specs.append(ctx(0))
        out_shape.append(jax.ShapeDtypeStruct((BATCH, CTX_LEN, w), BF16))
    outs = pl.pallas_call(
        functools.partial(_ret_kernel, need_ctx=need_ctx),
        grid=(BATCH,),
        in_specs=[pl.BlockSpec(memory_space=pltpu.SMEM)]
                 + [lat(j) for j in cols] + [ctx(j) for j in cols]
                 + [pl.BlockSpec((1, w), lambda b: (0, 0))],
        out_specs=out_specs,
        out_shape=out_shape,
        scratch_shapes=[
            pltpu.VMEM((RET_HEADS * RET_C, RET_C), F32),
            pltpu.VMEM((6, RET_C, w), F32),
            pltpu.VMEM((2, RET_NC, w, w), F32),
            pltpu.VMEM((2, RET_NC, w, w), BF16),
        ],
        compiler_params=_params(1),
        name="retention",
    )(ret_log_decay, p_lat, p_lat, p_lat, p_lat, p_ctx, p_ctx, p_ctx, p_ctx, gn_w)
    return (outs[0], outs[1]) if need_ctx else (outs[0], None)


DIFF_QT = 512
DIFF_SUB = 256
DIFF_KB = 64
ONES_ROWS = 16


def _diff_kernel(*refs, lam_init, n_sub, n_kv):
    q_ref = refs[0]
    k_refs = refs[1:1 + n_kv]
    vt_refs = refs[1 + n_kv:1 + 2 * n_kv]
    lam_ref, gnw_ref, zero_ref, o_ref, sbuf0, sbuf1, ebuf0, ebuf1 = refs[1 + 2 * n_kv:]
    sbufs, ebufs = (sbuf0, sbuf1), (ebuf0, ebuf1)
    sub = DIFF_SUB
    seg = [0]
    for k_ref in k_refs:
        seg.append(seg[-1] + k_ref.shape[0])
    keys = seg[-1]
    base = pl.multiple_of(zero_ref[0], LANES)
    lp = lam_ref[...]
    lam = (jnp.exp(jnp.sum(lp[0:1] * lp[1:2], axis=-1, keepdims=True))
           - jnp.exp(jnp.sum(lp[2:3] * lp[3:4], axis=-1, keepdims=True)) + lam_init)
    lane = lax.broadcasted_iota(jnp.int32, (sub, 2 * LANES), 1)
    grp = (lane % LANES) // 32
    items = [(qs, h) for qs in range(n_sub) for h in range(DIFF_HEADS)]

    def scores(idx, slot):
        qs, h = items[idx]
        hp, hl = divmod(h, 2)
        cols = slice(2 * LANES * hp, 2 * LANES * (hp + 1))
        qh = q_ref[qs * sub:(qs + 1) * sub, cols]
        q2 = jnp.concatenate([jnp.where(grp == 2 * hl + m, qh, jnp.zeros_like(qh))
                              for m in range(2)], axis=0)
        mx = None
        for j, k_ref in enumerate(k_refs):
            st = lax.dot_general(k_ref[:, cols], q2, (((1,), (1,)), ((), ())),
                                 preferred_element_type=F32)
            sbufs[slot][pl.ds(base + seg[j], seg[j + 1] - seg[j]), :] = st
            mj = jnp.max(st, axis=0, keepdims=True)
            mx = mj if mx is None else jnp.maximum(mx, mj)
        return mx

    def weights(slot, mx):
        for r0 in range(0, keys, DIFF_KB):
            rows = pl.ds(base + r0, DIFF_KB)
            x = (sbufs[slot][rows, :] - mx).astype(BF16)
            ebufs[slot][rows, :] = jnp.exp2(x)

    def output(idx, slot):
        qs, h = items[idx]
        ot = None
        for j, vt_ref in enumerate(vt_refs):
            n_j = seg[j + 1] - seg[j]
            va = jnp.concatenate([vt_ref[DIFF_DV * h:DIFF_DV * (h + 1), :],
                                  jnp.ones((ONES_ROWS, n_j), BF16)], axis=0)
            oj = jnp.dot(va, ebufs[slot][pl.ds(base + seg[j], n_j), :], preferred_element_type=F32)
            ot = oj if ot is None else ot + oj
        l = ot[DIFF_DV:DIFF_DV + 1, :]
        d = ot[:DIFF_DV, :sub] / l[:, :sub] - lam * (ot[:DIFF_DV, sub:] / l[:, sub:])
        ms = jnp.mean(d * d, axis=0, keepdims=True)
        y = (d * lax.rsqrt(ms + EPS)).T
        hcols = slice(DIFF_DV * h, DIFF_DV * (h + 1))
        o_ref[qs * sub:(qs + 1) * sub, hcols] = (y * gnw_ref[:, hcols] * (1.0 - lam_init)).astype(BF16)

    n = len(items)
    mxs = {0: scores(0, 0)}
    if n > 1:
        mxs[1] = scores(1, 1)
    weights(0, mxs.pop(0))
    for idx in range(n):
        if idx + 2 < n:
            mxs[idx + 2] = scores(idx + 2, idx % 2)
        if idx + 1 < n:
            weights((idx + 1) % 2, mxs.pop(idx + 1))
        output(idx, idx % 2)


def _diff_attention(p_q, p_kv, vt_kv, diff_lam, gn_w, lam_init, qt):
    w = DIFF_QK
    rows = p_q.shape[1]
    n_kv = len(p_kv)
    keys = sum(a.shape[1] for a in p_kv)
    return pl.pallas_call(
        functools.partial(_diff_kernel, lam_init=lam_init, n_sub=qt // DIFF_SUB, n_kv=n_kv),
        grid=(BATCH, rows // qt),
        in_specs=[pl.BlockSpec((None, qt, w), lambda b, t: (b, t, OFF_DQ // w))]
                 + [pl.BlockSpec((None, a.shape[1], w), lambda b, t: (b, 0, OFF_DK // w)) for a in p_kv]
                 + [pl.BlockSpec((None, DIFF_WIDTH, a.shape[2]), lambda b, t: (b, 0, 0)) for a in vt_kv]
                 + [pl.BlockSpec((4, DIFF_DK), lambda b, t: (0, 0)), pl.BlockSpec((1, w), lambda b, t: (0, 0)),
                    pl.BlockSpec(memory_space=pltpu.SMEM)],
        out_specs=pl.BlockSpec((None, qt, w), lambda b, t: (b, t, 0)),
        out_shape=jax.ShapeDtypeStruct((BATCH, rows, w), BF16),
        scratch_shapes=[pltpu.VMEM((keys, 2 * DIFF_SUB), F32)] * 2
                       + [pltpu.VMEM((keys, 2 * DIFF_SUB), BF16)] * 2,
        compiler_params=_params(2),
        name="diff_attn" if n_kv > 1 else "diff_attn_ctx",
    )(p_q, *p_kv, *vt_kv, diff_lam, gn_w, jnp.zeros((1,), jnp.int32))


CONV_GAP = 2 * SUBLANES
CONV_TILE = 128
CONV_WIN = CONV_TILE + 2 * SUBLANES
CONV_KW = 256


def _conv_rows(n):
    return CONV_GAP + n + 2 * CONV_GAP


def _conv_kernel(*refs, n_seq):
    ins = [refs[2 * i:2 * i + 2] for i in range(n_seq)]
    w_ref, b_ref, lnw_ref, lnb_ref = refs[2 * n_seq:2 * n_seq + 4]
    outs = refs[2 * n_seq + 4:3 * n_seq + 4]
    ubufs = refs[3 * n_seq + 4:4 * n_seq + 4]
    zbuf = refs[4 * n_seq + 4]
    ch = CONV_CH
    zbuf[...] = jnp.zeros_like(zbuf)

    kcol = lax.broadcasted_iota(jnp.int32, (CONV_TILE, SUBLANES * CONV_KW), 1)
    krow = lax.broadcasted_iota(jnp.int32, (CONV_TILE, SUBLANES * CONV_KW), 0)
    sel = jnp.where((kcol % CONV_KW) == krow + kcol // CONV_KW, 1.0, 0.0).astype(BF16)

    for (cv_ref, cg_ref), o_ref, ubuf in zip(ins, outs, ubufs):
        n = cv_ref.shape[0]
        ubuf[0:CONV_GAP, :] = jnp.zeros((CONV_GAP, ch), F32)
        ubuf[CONV_GAP + n:_conv_rows(n), :] = jnp.zeros((2 * CONV_GAP, ch), F32)
        ubuf[CONV_GAP:CONV_GAP + n, :] = cv_ref[...].astype(F32) * _sigmoid(cg_ref[...].astype(F32))
        for r0 in range(0, n, CONV_TILE):
            t0 = CONV_GAP + r0
            for r in range(SUBLANES):
                z = None
                for a in range(-2, 2):
                    o = SUBLANES * a + r
                    if abs(o) > CONV_PAD:
                        continue
                    lo = t0 + SUBLANES * a
                    term = ubuf[lo:lo + CONV_WIN, :] * w_ref[o + CONV_PAD:o + CONV_PAD + 1, :]
                    z = term if z is None else z + term
                zbuf[CONV_KW * r:CONV_KW * r + CONV_WIN, :] = z.astype(BF16)
            y = jnp.dot(sel, zbuf[...], preferred_element_type=F32) + b_ref[...]
            mu = jnp.mean(y, axis=-1, keepdims=True)
            dev = y - mu
            var = jnp.mean(dev * dev, axis=-1, keepdims=True)
            zn = dev * lax.rsqrt(var + EPS) * lnw_ref[...] + lnb_ref[...]
            o_ref[r0:r0 + CONV_TILE, :] = _silu(zn).astype(BF16)


def _conformer_conv(ps, conv_w, conv_b, ln_w, ln_b):
    ch = CONV_CH
    vec = pl.BlockSpec((1, ch), lambda b: (0, 0))
    in_specs, args = [], []
    for p in ps:
        n = p.shape[1]
        in_specs += [pl.BlockSpec((None, n, ch), lambda b: (b, 0, OFF_CV // ch)),
                     pl.BlockSpec((None, n, ch), lambda b: (b, 0, OFF_CG // ch))]
        args += [p, p]
    outs = pl.pallas_call(
        functools.partial(_conv_kernel, n_seq=len(ps)),
        grid=(BATCH,),
        in_specs=in_specs + [pl.BlockSpec((CONV_WIDTH, ch), lambda b: (0, 0)), vec, vec, vec],
        out_specs=[pl.BlockSpec((None, p.shape[1], ch), lambda b: (b, 0, 0)) for p in ps],
        out_shape=[jax.ShapeDtypeStruct((BATCH, p.shape[1], ch), BF16) for p in ps],
        scratch_shapes=[pltpu.VMEM((_conv_rows(p.shape[1]), ch), F32) for p in ps]
                       + [pltpu.VMEM((SUBLANES * CONV_KW, ch), BF16)],
        compiler_params=_params(1),
        name="conformer_conv",
    )(*args, conv_w, conv_b, ln_w, ln_b)
    return list(outs)


def _out_mlp_kernel(ret_ref, dif_ref, cnv_ref, x_ref, mod_ref, nw_ref, wo_ref, w1_ref, w2_ref,
                    fin_ref, o_ref, *, final):
    d = D_MODEL
    y = jnp.dot(ret_ref[...], wo_ref[0:RET_WIDTH, :], preferred_element_type=F32)
    y = y + jnp.dot(dif_ref[...], wo_ref[RET_WIDTH:RET_WIDTH + DIFF_WIDTH, :],
                    preferred_element_type=F32)
    y = y + jnp.dot(cnv_ref[...], wo_ref[RET_WIDTH + DIFF_WIDTH:MIX_WIDTH, :],
                    preferred_element_type=F32)
    x1 = x_ref[...] + mod_ref[:, 2 * d:3 * d] * y
    h2 = _rmsnorm_mod(x1, nw_ref[...], mod_ref[:, 3 * d:4 * d], mod_ref[:, 4 * d:5 * d]).astype(BF16)
    u = jnp.maximum(jnp.dot(h2, w1_ref[...], preferred_element_type=F32), 0.0)
    z = jnp.dot((u * u).astype(BF16), w2_ref[...], preferred_element_type=F32)
    x2 = x1 + mod_ref[:, 5 * d:6 * d] * z
    if final:
        ms = jnp.mean(x2 * x2, axis=-1, keepdims=True)
        x2 = x2 * lax.rsqrt(ms + EPS) * fin_ref[...]
    o_ref[...] = x2


def _out_mlp(ret, dif, cnv, xs, mod, nw, wo, w1, w2, layer, fin, tm, is_ctx, final):
    rows = xs.shape[1]
    const = lambda shape: pl.BlockSpec(shape, lambda b, t: (0,) * len(shape),
                                       pipeline_mode=pl.Buffered(1))
    weight = lambda shape: pl.BlockSpec((None,) + shape, lambda b, t: (layer, 0, 0),
                                        pipeline_mode=pl.Buffered(1))
    tile = lambda width: pl.BlockSpec((None, tm, width), lambda b, t: (b, t, 0))
    mod_row = (lambda b, t: (BATCH, 0, 0)) if is_ctx else (lambda b, t: (b, 0, 0))
    return pl.pallas_call(
        functools.partial(_out_mlp_kernel, final=final),
        grid=(BATCH, rows // tm),
        in_specs=[
            tile(RET_WIDTH), tile(DIFF_WIDTH), tile(CONV_CH), tile(D_MODEL),
            pl.BlockSpec((None, 1, N_MOD * D_MODEL), mod_row),
            const((1, D_MODEL)),
            weight((MIX_WIDTH, D_MODEL)), weight((D_MODEL, D_FF)), weight((D_FF, D_MODEL)),
            const((1, D_MODEL)),
        ],
        out_specs=tile(D_MODEL),
        out_shape=jax.ShapeDtypeStruct((BATCH, rows, D_MODEL), F32),
        compiler_params=_params(2),
        name="out_mlp_ctx" if is_ctx else "out_mlp",
    )(ret, dif, cnv, xs, mod, nw, wo, w1, w2, fin)


def kernel(x, c, ctx, c_ctx, norm1_w, norm2_w, ada_w, ada_b, w_in, ret_log_decay, ret_gn_w, diff_lam, diff_gn_w, conv_w, conv_b, conv_ln_w, conv_ln_b, w_out, mlp_w1, mlp_w2, final_norm_w):
    assert x.shape == (BATCH, SEQ, D_MODEL) and ctx.shape == (BATCH, CTX_LEN, D_MODEL)
    perm = _in_proj_permutation()
    tables = _rope_tables()
    cc = jnp.concatenate([c, c_ctx[None, :], jnp.zeros((MOD_ROWS - BATCH - 1, D_MODEL), F32)], axis=0)
    mod_all = _ada(cc, ada_w, ada_b).reshape(DEPTH, MOD_ROWS, 1, N_MOD * D_MODEL)
    fin = final_norm_w.reshape(1, D_MODEL)
    w_bf = w_in[:, :, perm].astype(BF16)
    wo, w1, w2 = w_out.astype(BF16), mlp_w1.astype(BF16), mlp_w2.astype(BF16)
    for i in range(DEPTH):
        last = i == DEPTH - 1
        lam_init = 0.8 - 0.6 * math.exp(-0.3 * i)
        mod = mod_all[i]
        nw1 = norm1_w[i].reshape(1, D_MODEL)
        nw2 = norm2_w[i].reshape(1, D_MODEL)
        dgn = diff_gn_w[i].reshape(1, DIFF_WIDTH)
        p_lat, vt_lat = _inproj(x, mod, nw1, w_bf, i, tables, TM_LAT)
        p_ctx, vt_ctx = _inproj(ctx, mod, nw1, w_bf, i, None, TM_CTX)
        ret_lat, ret_ctx = _retention(p_lat, p_ctx, ret_log_decay[i], ret_gn_w[i].reshape(1, RET_WIDTH),
                                      not last)
        dif_lat = _diff_attention(p_lat, [p_lat, p_ctx], [vt_lat, vt_ctx], diff_lam[i], dgn, lam_init,
                                  DIFF_QT)
        conv_args = (conv_w[i], conv_b[i].reshape(1, CONV_CH),
                     conv_ln_w[i].reshape(1, CONV_CH), conv_ln_b[i].reshape(1, CONV_CH))
        if last:
            cnv_lat, = _conformer_conv([p_lat], *conv_args)
        else:
            cnv_lat, cnv_ctx = _conformer_conv([p_lat, p_ctx], *conv_args)
            dif_ctx = _diff_attention(p_ctx, [p_ctx], [vt_ctx], diff_lam[i], dgn, lam_init, CTX_LEN)
            ctx = _out_mlp(ret_ctx, dif_ctx, cnv_ctx, ctx, mod, nw2, wo, w1, w2, i, fin, TM_CTX, True, False)
        x = _out_mlp(ret_lat, dif_lat, cnv_lat, x, mod, nw2, wo, w1, w2, i, fin, TM_LAT, False, last)
    return x
```

```python
import functools
import math

import jax
import jax.numpy as jnp
from jax import lax
from jax.experimental import pallas as pl
from jax.experimental.pallas import tpu as pltpu

D_MODEL = 1024
BATCH = 16
SEQ = 2048
DEPTH = 2
CTX_LEN = 256
GRID_W = 64
EPS = 1e-6
ROPE_BASE = 10000.0
RET_HEADS = 4
RET_DK = 64
RET_DV = 64
RET_QK = RET_HEADS * RET_DK
RET_WIDTH = RET_HEADS * RET_DV
DIFF_HEADS = 4
DIFF_DK = 64
DIFF_DV = 2 * DIFF_DK
DIFF_QK = DIFF_HEADS * 2 * DIFF_DK
DIFF_WIDTH = DIFF_HEADS * DIFF_DV
CONV_CH = D_MODEL // 4
CONV_WIDTH = 31
CONV_PAD = (CONV_WIDTH - 1) // 2
MIX_WIDTH = RET_WIDTH + DIFF_WIDTH + CONV_CH
IN_WIDTH = 2 * RET_QK + 2 * RET_WIDTH + 2 * DIFF_QK + DIFF_WIDTH + 2 * CONV_CH
D_FF = 4 * D_MODEL
N_MOD = 6

TOK = SEQ + CTX_LEN
LANES = 128
SUBLANES = 8
TM_LAT = 512
TM_CTX = CTX_LEN
MOD_ROWS = 24
VMEM_LIMIT = 56 * 1024 * 1024

OFF_RQ = 0
OFF_RK = OFF_RQ + RET_QK
OFF_RV = OFF_RK + RET_QK
OFF_RG = OFF_RV + RET_WIDTH
OFF_DQ = OFF_RG + RET_WIDTH
OFF_DK = OFF_DQ + DIFF_QK
OFF_CV = OFF_DK + DIFF_QK
OFF_CG = OFF_CV + CONV_CH
P_WIDTH = OFF_CG + CONV_CH
OFF_DV = P_WIDTH
SRC_DV = OFF_DK + DIFF_QK
SRC_CV = SRC_DV + DIFF_WIDTH

F32 = jnp.float32
BF16 = jnp.bfloat16
LOG2_E = math.log2(math.e)


def _params(n_axes):
    return pltpu.CompilerParams(dimension_semantics=("arbitrary",) * n_axes,
                                vmem_limit_bytes=VMEM_LIMIT)


def _sigmoid(x):
    return 1.0 / (1.0 + jnp.exp(-x))


def _silu(x):
    return x * _sigmoid(x)


def _reorder_in_proj(w):
    lead = w.shape[:2]
    rq, rk, rvg, dq, dk, dv, cvg = jnp.split(
        w, [OFF_RK, OFF_RV, OFF_DQ, OFF_DK, SRC_DV, SRC_CV], axis=2)

    def ret(a):
        half = RET_DK // 2
        return a.reshape(*lead, RET_HEADS, 2, half).transpose(0, 1, 3, 2, 4).reshape(*lead, RET_QK)

    def dif(a):
        q = DIFF_DK // 4
        return (a.reshape(*lead, DIFF_HEADS // 2, 2, 2, 2, 2, q).transpose(0, 1, 2, 6, 3, 4, 5, 7)
                .reshape(*lead, DIFF_QK))

    return jnp.concatenate([ret(rq), ret(rk), rvg, dif(dq), dif(dk), cvg, dv], axis=2)


def _rope_tables():
    def angles(pos, dim):
        inv = ROPE_BASE ** (-jnp.arange(0, dim, 2, dtype=F32) / dim)
        return pos.astype(F32)[:, None] * inv[None, :]

    t = jnp.arange(SEQ)
    ang_r = jnp.tile(angles(t, RET_DK), (1, RET_HEADS))
    half = DIFF_DK // 2
    ang_d = jnp.concatenate([angles(t // GRID_W, half), angles(t % GRID_W, half)], axis=1)
    ang_d = jnp.tile(ang_d, (1, 4))
    return jnp.cos(ang_r), jnp.sin(ang_r), jnp.cos(ang_d), jnp.sin(ang_d)


ADA_TN = 1536


def _ada_kernel(cc_ref, w_ref, b_ref, o_ref):
    s = _silu(cc_ref[...]).astype(BF16)
    o_ref[...] = jnp.dot(s, w_ref[...].astype(BF16), preferred_element_type=F32) + b_ref[...]


def _ada(cc, ada_w, ada_b):
    n = N_MOD * D_MODEL
    return pl.pallas_call(
        _ada_kernel,
        grid=(DEPTH, n // ADA_TN),
        in_specs=[
            pl.BlockSpec((MOD_ROWS, D_MODEL), lambda i, j: (0, 0)),
            pl.BlockSpec((None, D_MODEL, ADA_TN), lambda i, j: (i, 0, j)),
            pl.BlockSpec((None, 1, ADA_TN), lambda i, j: (i, 0, j)),
        ],
        out_specs=pl.BlockSpec((None, MOD_ROWS, ADA_TN), lambda i, j: (i, 0, j)),
        out_shape=jax.ShapeDtypeStruct((DEPTH, MOD_ROWS, n), F32),
        compiler_params=_params(2),
        name="ada_mod",
    )(cc, ada_w, ada_b.reshape(DEPTH, 1, n))


def _rmsnorm_mod(x, nw, shift, scale):
    ms = jnp.mean(x * x, axis=-1, keepdims=True)
    y = x * lax.rsqrt(ms + EPS) * nw
    return y * (1.0 + scale) + shift


def _inproj_kernel(*refs, rope):
    if rope:
        x_ref, mod_ref, nw_ref, w_ref, cr_ref, sr_ref, cd_ref, sd_ref, p_ref, vt_ref = refs
    else:
        x_ref, mod_ref, nw_ref, w_ref, p_ref, vt_ref = refs
    d = D_MODEL
    h = _rmsnorm_mod(x_ref[...], nw_ref[...], mod_ref[:, 0:d], mod_ref[:, d:2 * d]).astype(BF16)

    def proj(off, width):
        return jnp.dot(h, w_ref[:, off:off + width], preferred_element_type=F32)

    def rope_store(off, cos, sin, mult):
        z = proj(off, 2 * LANES)
        if rope:
            za, zb = z[:, :LANES], z[:, LANES:]
            p_ref[:, off:off + LANES] = ((za * cos - zb * sin) * mult).astype(BF16)
            p_ref[:, off + LANES:off + 2 * LANES] = ((zb * cos + za * sin) * mult).astype(BF16)
        else:
            p_ref[:, off:off + 2 * LANES] = (z * mult).astype(BF16)

    if rope:
        cr, sr, cd, sd = cr_ref[...], sr_ref[...], cd_ref[...], sd_ref[...]
    else:
        cr = sr = cd = sd = None
    rope_store(OFF_RQ, cr, sr, 1.0)
    rope_store(OFF_RK, cr, sr, RET_DK ** -0.5)
    for g in range(DIFF_QK // (2 * LANES)):
        rope_store(OFF_DQ + 2 * LANES * g, cd, sd, DIFF_DK ** -0.5 * LOG2_E)
        rope_store(OFF_DK + 2 * LANES * g, cd, sd, 1.0)
    for off, width in ((OFF_RV, 2 * RET_WIDTH), (OFF_CV, 2 * CONV_CH)):
        p_ref[:, off:off + width] = proj(off, width).astype(BF16)
    vt_ref[...] = proj(OFF_DV, DIFF_WIDTH).T.astype(BF16)


def _inproj(xs, mod, nw, w_bf, layer, tables, tm):
    rows = xs.shape[1]
    rope = tables is not None
    const = lambda shape: pl.BlockSpec(shape, lambda t, b: (0,) * len(shape))
    mod_row = (lambda t, b: (b, 0, 0)) if rope else (lambda t, b: (BATCH, 0, 0))
    in_specs = [
        pl.BlockSpec((None, tm, D_MODEL), lambda t, b: (b, t, 0)),
        pl.BlockSpec((None, 1, N_MOD * D_MODEL), mod_row),
        const((1, D_MODEL)),
        pl.BlockSpec((None, D_MODEL, IN_WIDTH), lambda t, b: (layer, 0, 0)),
    ]
    args = [xs, mod, nw, w_bf]
    if rope:
        in_specs += [pl.BlockSpec((tm, LANES), lambda t, b: (t, 0))] * 4
        args += list(tables)
    return pl.pallas_call(
        functools.partial(_inproj_kernel, rope=rope),
        grid=(rows // tm, BATCH),
        in_specs=in_specs,
        out_specs=[pl.BlockSpec((None, tm, P_WIDTH), lambda t, b: (b, t, 0)),
                   pl.BlockSpec((None, DIFF_WIDTH, tm), lambda t, b: (b, 0, t))],
        out_shape=[jax.ShapeDtypeStruct((BATCH, rows, P_WIDTH), BF16),
                   jax.ShapeDtypeStruct((BATCH, DIFF_WIDTH, rows), BF16)],
        compiler_params=_params(2),
        name="in_proj" if rope else "in_proj_ctx",
    )(*args)


RET_C = 256
RET_LAT_NC = SEQ // RET_C
RET_NC = RET_LAT_NC + CTX_LEN // RET_C
QD_F, QD_B, KD_F, KD_B, CD_F, CD_B = range(6)


def _ret_kernel(*refs, need_ctx):
    ld_ref, ql, kl, vl, gl, qc_, kc_, vc_, gc_, gnw_ref = refs[:10]
    if need_ctx:
        ol, oc, dtab, dvec, kvbuf, sbuf = refs[10:]
    else:
        ol, dtab, dvec, kvbuf, sbuf = refs[10:]
        oc = None
    c = RET_C
    w = RET_QK
    assert c == w

    def chunk(n):
        if n < RET_LAT_NC:
            return ql, kl, vl, gl, ol, slice(n * c, (n + 1) * c)
        return qc_, kc_, vc_, gc_, oc, slice(0, c)

    lane = lax.broadcasted_iota(jnp.int32, (c, w), 1)
    rowi = lax.broadcasted_iota(jnp.int32, (c, w), 0)
    row = rowi.astype(F32)
    hq = (lane % LANES) // 32
    hv = lane // RET_DV

    @pl.when(pl.program_id(0) == 0)
    def _():
        rel = rowi - lane
        rel_f = jnp.maximum(rel, 0).astype(F32)
        rel_b = jnp.maximum(-rel, 0).astype(F32)
        lgq = [jnp.zeros((c, w), F32), jnp.zeros((c, w), F32)]
        lgv = [jnp.zeros((c, w), F32), jnp.zeros((c, w), F32)]
        for h in range(RET_HEADS):
            lg = [-jnp.exp(jnp.full((c, w), ld_ref[d, h], F32)) for d in range(2)]
            dtab[h * c:(h + 1) * c, :] = jnp.where(
                rel > 0, jnp.exp(lg[0] * rel_f), jnp.where(rel < 0, jnp.exp(lg[1] * rel_b), 2.0))
            for d in range(2):
                lgq[d] = jnp.where(hq == h, lg[d], lgq[d])
                lgv[d] = jnp.where(hv == h, lg[d], lgv[d])
        dvec[QD_F] = jnp.exp(lgq[0] * (row + 1.0))
        dvec[QD_B] = jnp.exp(lgq[1] * (c - row))
        dvec[KD_F] = jnp.exp(lgq[0] * (c - 1.0 - row))
        dvec[KD_B] = jnp.exp(lgq[1] * row)
        dvec[CD_F] = jnp.exp(lgv[0] * float(c))
        dvec[CD_B] = jnp.exp(lgv[1] * float(c))

    same_head = ((rowi % LANES) // 32) == (lane // RET_DV)
    group_avg = jnp.where((rowi // RET_DV) == (lane // RET_DV), 1.0 / RET_DV, 0.0).astype(BF16)

    def hi_lo_dot(a, m):
        hi = a.astype(BF16)
        lo = (a - hi.astype(F32)).astype(BF16)
        return (jnp.dot(hi, m, preferred_element_type=F32)
                + jnp.dot(lo, m, preferred_element_type=F32))

    for n in range(RET_NC):
        _, k_ref, v_ref, _, _, rows = chunk(n)
        kc = k_ref[rows, :].astype(F32)
        k2 = jnp.concatenate([(kc * dvec[KD_F]).astype(BF16), (kc * dvec[KD_B]).astype(BF16)], axis=1)
        kv = lax.dot_general(k2, v_ref[rows, :], (((0,), (0,)), ((), ())),
                             preferred_element_type=F32)
        kvbuf[0, n] = jnp.where(same_head, kv[:w], 0.0)
        kvbuf[1, n] = jnp.where(same_head, kv[w:], 0.0)

    orders = ([RET_LAT_NC] + list(range(RET_LAT_NC)), [RET_LAT_NC] + list(range(RET_LAT_NC - 1, -1, -1)))
    for d, order in enumerate(orders):
        s = jnp.zeros((w, w), F32)
        for n in order[:-1]:
            sbuf[d, n] = s.astype(BF16)
            s = s * dvec[CD_F + d] + kvbuf[d, n]
        sbuf[d, order[-1]] = s.astype(BF16)

    for n in range(RET_NC if need_ctx else RET_LAT_NC):
        q_ref, k_ref, v_ref, g_ref, o_ref, rows = chunk(n)
        qc, kc, vc = q_ref[rows, :], k_ref[rows, :], v_ref[rows, :]
        q4 = jnp.concatenate([jnp.where(hq == h, qc, jnp.zeros_like(qc)) for h in range(RET_HEADS)],
                             axis=0)
        s = lax.dot_general(q4, kc, (((1,), (1,)), ((), ())), preferred_element_type=F32)
        pw = (s * dtab[...]).astype(BF16)
        qf = qc.astype(F32)
        lhs = jnp.concatenate([pw[h * c:(h + 1) * c] for h in range(RET_HEADS)]
                              + [(qf * dvec[QD_F]).astype(BF16), (qf * dvec[QD_B]).astype(BF16)], axis=1)
        rhs = jnp.concatenate([jnp.where(hv == h, vc, jnp.zeros_like(vc)) for h in range(RET_HEADS)]
                              + [sbuf[0, n], sbuf[1, n]], axis=0)
        o = jnp.dot(lhs, rhs, preferred_element_type=F32)
        mu = hi_lo_dot(o, group_avg)
        dev = o - mu
        var = hi_lo_dot(dev * dev, group_avg)
        t = dev * lax.rsqrt(var + EPS) * gnw_ref[...]
        g = g_ref[rows, :].astype(F32)
        o_ref[rows, :] = (t * _silu(g)).astype(BF16)


def _retention(p_lat, p_ctx, ret_log_decay, gn_w, need_ctx):
    w = RET_QK
    lat = lambda j: pl.BlockSpec((None, SEQ, w), lambda b: (b, 0, j))
    ctx = lambda j: pl.BlockSpec((None, CTX_LEN, w), lambda b: (b, 0, j))
    cols = (OFF_RQ // w, OFF_RK // w, OFF_RV // w, OFF_RG // w)
    out_specs = [lat(0)]
    out_shape = [jax.ShapeDtypeStruct((BATCH, SEQ, w), BF16)]
    if need_ctx:
        out_specs.append(ctx(0))
        out_shape.append(jax.ShapeDtypeStruct((BATCH, CTX_LEN, w), BF16))
    outs = pl.pallas_call(
        functools.partial(_ret_kernel, need_ctx=need_ctx),
        grid=(BATCH,),
        in_specs=[pl.BlockSpec(memory_space=pltpu.SMEM)]
                 + [lat(j) for j in cols] + [ctx(j) for j in cols]
                 + [pl.BlockSpec((1, w), lambda b: (0, 0))],
        out_specs=out_specs,
        out_shape=out_shape,
        scratch_shapes=[
            pltpu.VMEM((RET_HEADS * RET_C, RET_C), F32),
            pltpu.VMEM((6, RET_C, w), F32),
            pltpu.VMEM((2, RET_NC, w, w), F32),
            pltpu.VMEM((2, RET_NC, w, w), BF16),
        ],
        compiler_params=_params(1),
        name="retention",
    )(ret_log_decay, p_lat, p_lat, p_lat, p_lat, p_ctx, p_ctx, p_ctx, p_ctx, gn_w)
    return (outs[0], outs[1]) if need_ctx else (outs[0], None)


DIFF_QT = 512
DIFF_SUB = 256
DIFF_KB = 64
ONES_ROWS = 16


def _diff_kernel(*refs, lam_init, n_sub, n_kv):
    q_ref = refs[0]
    k_refs = refs[1:1 + n_kv]
    vt_refs = refs[1 + n_kv:1 + 2 * n_kv]
    lam_ref, gnw_ref, zero_ref, o_ref, sbuf0, sbuf1, ebuf0, ebuf1 = refs[1 + 2 * n_kv:]
    sbufs, ebufs = (sbuf0, sbuf1), (ebuf0, ebuf1)
    sub = DIFF_SUB
    seg = [0]
    for k_ref in k_refs:
        seg.append(seg[-1] + k_ref.shape[0])
    keys = seg[-1]
    base = pl.multiple_of(zero_ref[0], LANES)
    lp = lam_ref[...]
    lam = (jnp.exp(jnp.sum(lp[0:1] * lp[1:2], axis=-1, keepdims=True))
           - jnp.exp(jnp.sum(lp[2:3] * lp[3:4], axis=-1, keepdims=True)) + lam_init)
    lane = lax.broadcasted_iota(jnp.int32, (sub, 2 * LANES), 1)
    grp = (lane % LANES) // 32
    items = [(qs, h) for qs in range(n_sub) for h in range(DIFF_HEADS)]

    def scores(idx, slot):
        qs, h = items[idx]
        hp, hl = divmod(h, 2)
        cols = slice(2 * LANES * hp, 2 * LANES * (hp + 1))
        qh = q_ref[qs * sub:(qs + 1) * sub, cols]
        q2 = jnp.concatenate([jnp.where(grp == 2 * hl + m, qh, jnp.zeros_like(qh))
                              for m in range(2)], axis=0)
        mx = None
        for j, k_ref in enumerate(k_refs):
            st = lax.dot_general(k_ref[:, cols], q2, (((1,), (1,)), ((), ())),
                                 preferred_element_type=F32)
            sbufs[slot][pl.ds(base + seg[j], seg[j + 1] - seg[j]), :] = st
            mj = jnp.max(st, axis=0, keepdims=True)
            mx = mj if mx is None else jnp.maximum(mx, mj)
        return mx

    def weights(slot, mx):
        for r0 in range(0, keys, DIFF_KB):
            rows = pl.ds(base + r0, DIFF_KB)
            x = (sbufs[slot][rows, :] - mx).astype(BF16)
            ebufs[slot][rows, :] = jnp.exp2(x)

    def output(idx, slot):
        qs, h = items[idx]
        ot = None
        for j, vt_ref in enumerate(vt_refs):
            n_j = seg[j + 1] - seg[j]
            va = jnp.concatenate([vt_ref[DIFF_DV * h:DIFF_DV * (h + 1), :],
                                  jnp.ones((ONES_ROWS, n_j), BF16)], axis=0)
            oj = jnp.dot(va, ebufs[slot][pl.ds(base + seg[j], n_j), :], preferred_element_type=F32)
            ot = oj if ot is None else ot + oj
        l = ot[DIFF_DV:DIFF_DV + 1, :]
        d = ot[:DIFF_DV, :sub] / l[:, :sub] - lam * (ot[:DIFF_DV, sub:] / l[:, sub:])
        ms = jnp.mean(d * d, axis=0, keepdims=True)
        y = (d * lax.rsqrt(ms + EPS)).T
        hcols = slice(DIFF_DV * h, DIFF_DV * (h + 1))
        o_ref[qs * sub:(qs + 1) * sub, hcols] = (y * gnw_ref[:, hcols] * (1.0 - lam_init)).astype(BF16)

    n = len(items)
    mxs = {0: scores(0, 0)}
    if n > 1:
        mxs[1] = scores(1, 1)
    weights(0, mxs.pop(0))
    for idx in range(n):
        if idx + 2 < n:
            mxs[idx + 2] = scores(idx + 2, idx % 2)
        if idx + 1 < n:
            weights((idx + 1) % 2, mxs.pop(idx + 1))
        output(idx, idx % 2)


def _diff_attention(p_q, p_kv, vt_kv, diff_lam, gn_w, lam_init, qt):
    w = DIFF_QK
    rows = p_q.shape[1]
    n_kv = len(p_kv)
    keys = sum(a.shape[1] for a in p_kv)
    return pl.pallas_call(
        functools.partial(_diff_kernel, lam_init=lam_init, n_sub=qt // DIFF_SUB, n_kv=n_kv),
        grid=(BATCH, rows // qt),
        in_specs=[pl.BlockSpec((None, qt, w), lambda b, t: (b, t, OFF_DQ // w))]
                 + [pl.BlockSpec((None, a.shape[1], w), lambda b, t: (b, 0, OFF_DK // w)) for a in p_kv]
                 + [pl.BlockSpec((None, DIFF_WIDTH, a.shape[2]), lambda b, t: (b, 0, 0)) for a in vt_kv]
                 + [pl.BlockSpec((4, DIFF_DK), lambda b, t: (0, 0)), pl.BlockSpec((1, w), lambda b, t: (0, 0)),
                    pl.BlockSpec(memory_space=pltpu.SMEM)],
        out_specs=pl.BlockSpec((None, qt, w), lambda b, t: (b, t, 0)),
        out_shape=jax.ShapeDtypeStruct((BATCH, rows, w), BF16),
        scratch_shapes=[pltpu.VMEM((keys, 2 * DIFF_SUB), F32)] * 2
                       + [pltpu.VMEM((keys, 2 * DIFF_SUB), BF16)] * 2,
        compiler_params=_params(2),
        name="diff_attn" if n_kv > 1 else "diff_attn_ctx",
    )(p_q, *p_kv, *vt_kv, diff_lam, gn_w, jnp.zeros((1,), jnp.int32))


CONV_RES = 2 * SUBLANES
CONV_GAP = CONV_RES
CONV_TILE = 128
CONV_WIN = CONV_TILE + CONV_RES


def _conv_rows(n):
    return CONV_GAP + n + CONV_GAP


def _conv_kernel(*refs, n_seq):
    ins = [refs[2 * i:2 * i + 2] for i in range(n_seq)]
    w_ref, b_ref, lnw_ref, lnb_ref = refs[2 * n_seq:2 * n_seq + 4]
    outs = refs[2 * n_seq + 4:3 * n_seq + 4]
    ubufs = refs[3 * n_seq + 4:4 * n_seq + 4]
    zbuf = refs[4 * n_seq + 4]
    ch = CONV_CH
    wb = w_ref[...].astype(BF16)

    kcol = lax.broadcasted_iota(jnp.int32, (CONV_TILE, CONV_RES * CONV_WIN), 1)
    krow = lax.broadcasted_iota(jnp.int32, (CONV_TILE, CONV_RES * CONV_WIN), 0)
    sel = jnp.where((kcol % CONV_WIN) == krow + kcol // CONV_WIN, 1.0, 0.0).astype(BF16)

    for (cv_ref, cg_ref), o_ref, ubuf in zip(ins, outs, ubufs):
        n = cv_ref.shape[0]
        ubuf[0:CONV_GAP, :] = jnp.zeros((CONV_GAP, ch), BF16)
        ubuf[CONV_GAP + n:_conv_rows(n), :] = jnp.zeros((CONV_GAP, ch), BF16)
        ubuf[CONV_GAP:CONV_GAP + n, :] = (cv_ref[...].astype(F32)
                                          * _sigmoid(cg_ref[...].astype(F32))).astype(BF16)
        for r0 in range(0, n, CONV_TILE):
            t0 = CONV_GAP + r0
            for r in range(CONV_RES):
                z = ubuf[t0:t0 + CONV_WIN, :] * wb[r + CONV_PAD:r + CONV_PAD + 1, :]
                if r >= 1:
                    z = z + (ubuf[t0 - CONV_RES:t0 - CONV_RES + CONV_WIN, :]
                             * wb[r - CONV_RES + CONV_PAD:r - CONV_RES + CONV_PAD + 1, :])
                zbuf[CONV_WIN * r:CONV_WIN * (r + 1), :] = z
            y = jnp.dot(sel, zbuf[...], preferred_element_type=F32) + b_ref[...]
            mu = jnp.mean(y, axis=-1, keepdims=True)
            dev = y - mu
            var = jnp.mean(dev * dev, axis=-1, keepdims=True)
            zn = dev * lax.rsqrt(var + EPS) * lnw_ref[...] + lnb_ref[...]
            o_ref[r0:r0 + CONV_TILE, :] = _silu(zn).astype(BF16)


def _conformer_conv(ps, conv_w, conv_b, ln_w, ln_b):
    ch = CONV_CH
    vec = pl.BlockSpec((1, ch), lambda b: (0, 0))
    in_specs, args = [], []
    for p in ps:
        n = p.shape[1]
        in_specs += [pl.BlockSpec((None, n, ch), lambda b: (b, 0, OFF_CV // ch)),
                     pl.BlockSpec((None, n, ch), lambda b: (b, 0, OFF_CG // ch))]
        args += [p, p]
    outs = pl.pallas_call(
        functools.partial(_conv_kernel, n_seq=len(ps)),
        grid=(BATCH,),
        in_specs=in_specs + [pl.BlockSpec((CONV_WIDTH, ch), lambda b: (0, 0)), vec, vec, vec],
        out_specs=[pl.BlockSpec((None, p.shape[1], ch), lambda b: (b, 0, 0)) for p in ps],
        out_shape=[jax.ShapeDtypeStruct((BATCH, p.shape[1], ch), BF16) for p in ps],
        scratch_shapes=[pltpu.VMEM((_conv_rows(p.shape[1]), ch), BF16) for p in ps]
                       + [pltpu.VMEM((CONV_RES * CONV_WIN, ch), BF16)],
        compiler_params=_params(1),
        name="conformer_conv",
    )(*args, conv_w, conv_b, ln_w, ln_b)
    return list(outs)


def _out_mlp_kernel(ret_ref, dif_ref, cnv_ref, x_ref, mod_ref, nw_ref, wo_ref, w1_ref, w2_ref,
                    fin_ref, o_ref, *, final):
    d = D_MODEL
    y = jnp.dot(ret_ref[...], wo_ref[0:RET_WIDTH, :], preferred_element_type=F32)
    y = y + jnp.dot(dif_ref[...], wo_ref[RET_WIDTH:RET_WIDTH + DIFF_WIDTH, :],
                    preferred_element_type=F32)
    y = y + jnp.dot(cnv_ref[...], wo_ref[RET_WIDTH + DIFF_WIDTH:MIX_WIDTH, :],
                    preferred_element_type=F32)
    x1 = x_ref[...] + mod_ref[:, 2 * d:3 * d] * y
    h2 = _rmsnorm_mod(x1, nw_ref[...], mod_ref[:, 3 * d:4 * d], mod_ref[:, 4 * d:5 * d]).astype(BF16)
    u = jnp.maximum(jnp.dot(h2, w1_ref[...], preferred_element_type=F32), 0.0)
    z = jnp.dot((u * u).astype(BF16), w2_ref[...], preferred_element_type=F32)
    x2 = x1 + mod_ref[:, 5 * d:6 * d] * z
    if final:
        ms = jnp.mean(x2 * x2, axis=-1, keepdims=True)
        x2 = x2 * lax.rsqrt(ms + EPS) * fin_ref[...]
    o_ref[...] = x2


def _out_mlp(ret, dif, cnv, xs, mod, nw, wo, w1, w2, layer, fin, tm, is_ctx, final):
    rows = xs.shape[1]
    const = lambda shape: pl.BlockSpec(shape, lambda b, t: (0,) * len(shape),
                                       pipeline_mode=pl.Buffered(1))
    weight = lambda shape: pl.BlockSpec((None,) + shape, lambda b, t: (layer, 0, 0),
                                        pipeline_mode=pl.Buffered(1))
    tile = lambda width: pl.BlockSpec((None, tm, width), lambda b, t: (b, t, 0))
    mod_row = (lambda b, t: (BATCH, 0, 0)) if is_ctx else (lambda b, t: (b, 0, 0))
    return pl.pallas_call(
        functools.partial(_out_mlp_kernel, final=final),
        grid=(BATCH, rows // tm),
        in_specs=[
            tile(RET_WIDTH), tile(DIFF_WIDTH), tile(CONV_CH), tile(D_MODEL),
            pl.BlockSpec((None, 1, N_MOD * D_MODEL), mod_row),
            const((1, D_MODEL)),
            weight((MIX_WIDTH, D_MODEL)), weight((D_MODEL, D_FF)), weight((D_FF, D_MODEL)),
            const((1, D_MODEL)),
        ],
        out_specs=tile(D_MODEL),
        out_shape=jax.ShapeDtypeStruct((BATCH, rows, D_MODEL), F32),
        compiler_params=_params(2),
        name="out_mlp_ctx" if is_ctx else "out_mlp",
    )(ret, dif, cnv, xs, mod, nw, wo, w1, w2, fin)


def kernel(x, c, ctx, c_ctx, norm1_w, norm2_w, ada_w, ada_b, w_in, ret_log_decay, ret_gn_w, diff_lam, diff_gn_w, conv_w, conv_b, conv_ln_w, conv_ln_b, w_out, mlp_w1, mlp_w2, final_norm_w):
    assert x.shape == (BATCH, SEQ, D_MODEL) and ctx.shape == (BATCH, CTX_LEN, D_MODEL)
    tables = _rope_tables()
    cc = jnp.concatenate([c, c_ctx[None, :], jnp.zeros((MOD_ROWS - BATCH - 1, D_MODEL), F32)], axis=0)
    mod_all = _ada(cc, ada_w, ada_b).reshape(DEPTH, MOD_ROWS, 1, N_MOD * D_MODEL)
    fin = final_norm_w.reshape(1, D_MODEL)
    w_bf = _reorder_in_proj(w_in).astype(BF16)
    wo, w1, w2 = w_out.astype(BF16), mlp_w1.astype(BF16), mlp_w2.astype(BF16)
    for i in range(DEPTH):
        last = i == DEPTH - 1
        lam_init = 0.8 - 0.6 * math.exp(-0.3 * i)
        mod = mod_all[i]
        nw1 = norm1_w[i].reshape(1, D_MODEL)
        nw2 = norm2_w[i].reshape(1, D_MODEL)
        dgn = diff_gn_w[i].reshape(1, DIFF_WIDTH)
        p_lat, vt_lat = _inproj(x, mod, nw1, w_bf, i, tables, TM_LAT)
        p_ctx, vt_ctx = _inproj(ctx, mod, nw1, w_bf, i, None, TM_CTX)
        ret_lat, ret_ctx = _retention(p_lat, p_ctx, ret_log_decay[i], ret_gn_w[i].reshape(1, RET_WIDTH),
                                      not last)
        dif_lat = _diff_attention(p_lat, [p_lat, p_ctx], [vt_lat, vt_ctx], diff_lam[i], dgn, lam_init,
                                  DIFF_QT)
        conv_args = (conv_w[i], conv_b[i].reshape(1, CONV_CH),
                     conv_ln_w[i].reshape(1, CONV_CH), conv_ln_b[i].reshape(1, CONV_CH))
        if last:
            cnv_lat, = _conformer_conv([p_lat], *conv_args)
        else:
            cnv_lat, cnv_ctx = _conformer_conv([p_lat, p_ctx], *conv_args)
            dif_ctx = _diff_attention(p_ctx, [p_ctx], [vt_ctx], diff_lam[i], dgn, lam_init, CTX_LEN)
            ctx = _out_mlp(ret_ctx, dif_ctx, cnv_ctx, ctx, mod, nw2, wo, w1, w2, i, fin, TM_CTX, True, False)
        x = _out_mlp(ret_lat, dif_lat, cnv_lat, x, mod, nw2, wo, w1, w2, i, fin, TM_LAT, False, last)
    return x
```

```python
import functools
import math

import jax
import jax.numpy as jnp
from jax import lax
from jax.experimental import pallas as pl
from jax.experimental.pallas import tpu as pltpu

D_MODEL = 1024
BATCH = 16
SEQ = 2048
DEPTH = 2
CTX_LEN = 256
GRID_W = 64
EPS = 1e-6
ROPE_BASE = 10000.0
RET_HEADS = 4
RET_DK = 64
RET_DV = 64
RET_QK = RET_HEADS * RET_DK
RET_WIDTH = RET_HEADS * RET_DV
DIFF_HEADS = 4
DIFF_DK = 64
DIFF_DV = 2 * DIFF_DK
DIFF_QK = DIFF_HEADS * 2 * DIFF_DK
DIFF_WIDTH = DIFF_HEADS * DIFF_DV
CONV_CH = D_MODEL // 4
CONV_WIDTH = 31
CONV_PAD = (CONV_WIDTH - 1) // 2
MIX_WIDTH = RET_WIDTH + DIFF_WIDTH + CONV_CH
IN_WIDTH = 2 * RET_QK + 2 * RET_WIDTH + 2 * DIFF_QK + DIFF_WIDTH + 2 * CONV_CH
D_FF = 4 * D_MODEL
N_MOD = 6

TOK = SEQ + CTX_LEN
LANES = 128
SUBLANES = 8
TM_LAT = 512
TM_CTX = CTX_LEN
MOD_ROWS = 24
VMEM_LIMIT = 56 * 1024 * 1024

OFF_RQ = 0
OFF_RK = OFF_RQ + RET_QK
OFF_RV = OFF_RK + RET_QK
OFF_RG = OFF_RV + RET_WIDTH
OFF_DQ = OFF_RG + RET_WIDTH
OFF_DK = OFF_DQ + DIFF_QK
OFF_CV = OFF_DK + DIFF_QK
OFF_CG = OFF_CV + CONV_CH
P_WIDTH = OFF_CG + CONV_CH
OFF_DV = P_WIDTH
SRC_DV = OFF_DK + DIFF_QK
SRC_CV = SRC_DV + DIFF_WIDTH

F32 = jnp.float32
BF16 = jnp.bfloat16
LOG2_E = math.log2(math.e)


def _params(n_axes):
    return pltpu.CompilerParams(dimension_semantics=("arbitrary",) * n_axes,
                                vmem_limit_bytes=VMEM_LIMIT)


def _sigmoid(x):
    return 1.0 / (1.0 + jnp.exp(-x))


def _silu(x):
    return x * _sigmoid(x)


def _reorder_in_proj(w):
    lead = w.shape[:2]
    rq, rk, rvg, dq, dk, dv, cvg = jnp.split(
        w, [OFF_RK, OFF_RV, OFF_DQ, OFF_DK, SRC_DV, SRC_CV], axis=2)

    def ret(a):
        half = RET_DK // 2
        return a.reshape(*lead, RET_HEADS, 2, half).transpose(0, 1, 3, 2, 4).reshape(*lead, RET_QK)

    def dif(a):
        q = DIFF_DK // 4
        return (a.reshape(*lead, DIFF_HEADS // 2, 2, 2, 2, 2, q).transpose(0, 1, 2, 6, 3, 4, 5, 7)
                .reshape(*lead, DIFF_QK))

    return jnp.concatenate([ret(rq), ret(rk), rvg, dif(dq), dif(dk), cvg, dv], axis=2)


def _rope_tables():
    def angles(pos, dim):
        inv = ROPE_BASE ** (-jnp.arange(0, dim, 2, dtype=F32) / dim)
        return pos.astype(F32)[:, None] * inv[None, :]

    t = jnp.arange(SEQ)
    ang_r = jnp.tile(angles(t, RET_DK), (1, RET_HEADS))
    half = DIFF_DK // 2
    ang_d = jnp.concatenate([angles(t // GRID_W, half), angles(t % GRID_W, half)], axis=1)
    ang_d = jnp.tile(ang_d, (1, 4))
    return jnp.cos(ang_r), jnp.sin(ang_r), jnp.cos(ang_d), jnp.sin(ang_d)


ADA_TN = 1536


def _ada_kernel(cc_ref, w_ref, b_ref, o_ref):
    s = _silu(cc_ref[...]).astype(BF16)
    o_ref[...] = jnp.dot(s, w_ref[...].astype(BF16), preferred_element_type=F32) + b_ref[...]


def _ada(cc, ada_w, ada_b):
    n = N_MOD * D_MODEL
    return pl.pallas_call(
        _ada_kernel,
        grid=(DEPTH, n // ADA_TN),
        in_specs=[
            pl.BlockSpec((MOD_ROWS, D_MODEL), lambda i, j: (0, 0)),
            pl.BlockSpec((None, D_MODEL, ADA_TN), lambda i, j: (i, 0, j)),
            pl.BlockSpec((None, 1, ADA_TN), lambda i, j: (i, 0, j)),
        ],
        out_specs=pl.BlockSpec((None, MOD_ROWS, ADA_TN), lambda i, j: (i, 0, j)),
        out_shape=jax.ShapeDtypeStruct((DEPTH, MOD_ROWS, n), F32),
        compiler_params=_params(2),
        name="ada_mod",
    )(cc, ada_w, ada_b.reshape(DEPTH, 1, n))


def _rmsnorm_mod(x, nw, shift, scale):
    ms = jnp.mean(x * x, axis=-1, keepdims=True)
    y = x * lax.rsqrt(ms + EPS) * nw
    return y * (1.0 + scale) + shift


def _inproj_kernel(*refs, rope):
    if rope:
        x_ref, mod_ref, nw_ref, w_ref, cr_ref, sr_ref, cd_ref, sd_ref, p_ref, vt_ref = refs
    else:
        x_ref, mod_ref, nw_ref, w_ref, p_ref, vt_ref = refs
    d = D_MODEL
    h = _rmsnorm_mod(x_ref[...], nw_ref[...], mod_ref[:, 0:d], mod_ref[:, d:2 * d]).astype(BF16)

    def proj(off, width):
        return jnp.dot(h, w_ref[:, off:off + width], preferred_element_type=F32)

    def rope_store(off, cos, sin, mult):
        z = proj(off, 2 * LANES)
        if rope:
            za, zb = z[:, :LANES], z[:, LANES:]
            p_ref[:, off:off + LANES] = ((za * cos - zb * sin) * mult).astype(BF16)
            p_ref[:, off + LANES:off + 2 * LANES] = ((zb * cos + za * sin) * mult).astype(BF16)
        else:
            p_ref[:, off:off + 2 * LANES] = (z * mult).astype(BF16)

    if rope:
        cr, sr, cd, sd = cr_ref[...], sr_ref[...], cd_ref[...], sd_ref[...]
    else:
        cr = sr = cd = sd = None
    rope_store(OFF_RQ, cr, sr, 1.0)
    rope_store(OFF_RK, cr, sr, RET_DK ** -0.5)
    for g in range(DIFF_QK // (2 * LANES)):
        rope_store(OFF_DQ + 2 * LANES * g, cd, sd, DIFF_DK ** -0.5 * LOG2_E)
        rope_store(OFF_DK + 2 * LANES * g, cd, sd, 1.0)
    for off, width in ((OFF_RV, 2 * RET_WIDTH), (OFF_CV, 2 * CONV_CH)):
        p_ref[:, off:off + width] = proj(off, width).astype(BF16)
    vt_ref[...] = proj(OFF_DV, DIFF_WIDTH).T.astype(BF16)


def _inproj(xs, mod, nw, w_bf, layer, tables, tm):
    rows = xs.shape[1]
    rope = tables is not None
    const = lambda shape: pl.BlockSpec(shape, lambda t, b: (0,) * len(shape))
    mod_row = (lambda t, b: (b, 0, 0)) if rope else (lambda t, b: (BATCH, 0, 0))
    in_specs = [
        pl.BlockSpec((None, tm, D_MODEL), lambda t, b: (b, t, 0)),
        pl.BlockSpec((None, 1, N_MOD * D_MODEL), mod_row),
        const((1, D_MODEL)),
        pl.BlockSpec((None, D_MODEL, IN_WIDTH), lambda t, b: (layer, 0, 0)),
    ]
    args = [xs, mod, nw, w_bf]
    if rope:
        in_specs += [pl.BlockSpec((tm, LANES), lambda t, b: (t, 0))] * 4
        args += list(tables)
    return pl.pallas_call(
        functools.partial(_inproj_kernel, rope=rope),
        grid=(rows // tm, BATCH),
        in_specs=in_specs,
        out_specs=[pl.BlockSpec((None, tm, P_WIDTH), lambda t, b: (b, t, 0)),
                   pl.BlockSpec((None, DIFF_WIDTH, tm), lambda t, b: (b, 0, t))],
        out_shape=[jax.ShapeDtypeStruct((BATCH, rows, P_WIDTH), BF16),
                   jax.ShapeDtypeStruct((BATCH, DIFF_WIDTH, rows), BF16)],
        compiler_params=_params(2),
        name="in_proj" if rope else "in_proj_ctx",
    )(*args)


RET_C = 256
RET_LAT_NC = SEQ // RET_C
RET_NC = RET_LAT_NC + CTX_LEN // RET_C
QD_F, QD_B, KD_F, KD_B, CD_F, CD_B = range(6)


def _ret_kernel(*refs, need_ctx):
    ld_ref, ql, kl, vl, gl, qc_, kc_, vc_, gc_, gnw_ref = refs[:10]
    if need_ctx:
        ol, oc, dtab, dvec, kvbuf, sbuf, pwbuf, obuf = refs[10:]
    else:
        ol, dtab, dvec, kvbuf, sbuf, pwbuf, obuf = refs[10:]
        oc = None
    c = RET_C
    w = RET_QK
    assert c == w

    def chunk(n):
        if n < RET_LAT_NC:
            return ql, kl, vl, gl, ol, slice(n * c, (n + 1) * c)
        return qc_, kc_, vc_, gc_, oc, slice(0, c)

    lane = lax.broadcasted_iota(jnp.int32, (c, w), 1)
    rowi = lax.broadcasted_iota(jnp.int32, (c, w), 0)
    row = rowi.astype(F32)
    hq = (lane % LANES) // 32
    hv = lane // RET_DV

    @pl.when(pl.program_id(0) == 0)
    def _():
        rel = rowi - lane
        rel_f = jnp.maximum(rel, 0).astype(F32)
        rel_b = jnp.maximum(-rel, 0).astype(F32)
        lgq = [jnp.zeros((c, w), F32), jnp.zeros((c, w), F32)]
        lgv = [jnp.zeros((c, w), F32), jnp.zeros((c, w), F32)]
        for h in range(RET_HEADS):
            lg = [-jnp.exp(jnp.full((c, w), ld_ref[d, h], F32)) for d in range(2)]
            dtab[h * c:(h + 1) * c, :] = jnp.where(
                rel > 0, jnp.exp(lg[0] * rel_f), jnp.where(rel < 0, jnp.exp(lg[1] * rel_b), 2.0))
            for d in range(2):
                lgq[d] = jnp.where(hq == h, lg[d], lgq[d])
                lgv[d] = jnp.where(hv == h, lg[d], lgv[d])
        dvec[QD_F] = jnp.exp(lgq[0] * (row + 1.0))
        dvec[QD_B] = jnp.exp(lgq[1] * (c - row))
        dvec[KD_F] = jnp.exp(lgq[0] * (c - 1.0 - row))
        dvec[KD_B] = jnp.exp(lgq[1] * row)
        dvec[CD_F] = jnp.exp(lgv[0] * float(c))
        dvec[CD_B] = jnp.exp(lgv[1] * float(c))

    same_head = ((rowi % LANES) // 32) == (lane // RET_DV)
    group_avg = jnp.where((rowi // RET_DV) == (lane // RET_DV), 1.0 / RET_DV, 0.0).astype(BF16)

    def hi_lo_dot(a, m):
        hi = a.astype(BF16)
        lo = (a - hi.astype(F32)).astype(BF16)
        return (jnp.dot(hi, m, preferred_element_type=F32)
                + jnp.dot(lo, m, preferred_element_type=F32))

    for n in range(RET_NC):
        _, k_ref, v_ref, _, _, rows = chunk(n)
        kc = k_ref[rows, :].astype(F32)
        k2 = jnp.concatenate([(kc * dvec[KD_F]).astype(BF16), (kc * dvec[KD_B]).astype(BF16)], axis=1)
        kv = lax.dot_general(k2, v_ref[rows, :], (((0,), (0,)), ((), ())),
                             preferred_element_type=F32)
        kvbuf[0, n] = jnp.where(same_head, kv[:w], 0.0)
        kvbuf[1, n] = jnp.where(same_head, kv[w:], 0.0)

    orders = ([RET_LAT_NC] + list(range(RET_LAT_NC)), [RET_LAT_NC] + list(range(RET_LAT_NC - 1, -1, -1)))
    for d, order in enumerate(orders):
        s = jnp.zeros((w, w), F32)
        for n in order[:-1]:
            sbuf[d, n] = s.astype(BF16)
            s = s * dvec[CD_F + d] + kvbuf[d, n]
        sbuf[d, order[-1]] = s.astype(BF16)

    out_chunks = range(RET_NC if need_ctx else RET_LAT_NC)
    for n in out_chunks:
        q_ref, k_ref, _, _, _, rows = chunk(n)
        qc = q_ref[rows, :]
        q4 = jnp.concatenate([jnp.where(hq == h, qc, jnp.zeros_like(qc)) for h in range(RET_HEADS)],
                             axis=0)
        s = lax.dot_general(q4, k_ref[rows, :], (((1,), (1,)), ((), ())),
                            preferred_element_type=F32)
        pwbuf[n] = (s * dtab[...]).astype(BF16)
    for n in out_chunks:
        q_ref, _, v_ref, _, _, rows = chunk(n)
        vc = v_ref[rows, :]
        qf = q_ref[rows, :].astype(F32)
        lhs = jnp.concatenate([pwbuf[n, h * c:(h + 1) * c] for h in range(RET_HEADS)]
                              + [(qf * dvec[QD_F]).astype(BF16), (qf * dvec[QD_B]).astype(BF16)], axis=1)
        rhs = jnp.concatenate([jnp.where(hv == h, vc, jnp.zeros_like(vc)) for h in range(RET_HEADS)]
                              + [sbuf[0, n], sbuf[1, n]], axis=0)
        obuf[n] = jnp.dot(lhs, rhs, preferred_element_type=F32)
    for n in out_chunks:
        _, _, _, g_ref, o_ref, rows = chunk(n)
        o = obuf[n]
        mu = hi_lo_dot(o, group_avg)
        dev = o - mu
        var = hi_lo_dot(dev * dev, group_avg)
        t = dev * lax.rsqrt(var + EPS) * gnw_ref[...]
        g = g_ref[rows, :].astype(F32)
        o_ref[rows, :] = (t * _silu(g)).astype(BF16)


DIFF_QT = 512
DIFF_SUB = 256
DIFF_KB = 64
ONES_ROWS = 16


def _diff_kernel(*refs, lam_init, n_sub, n_kv):
    q_ref = refs[0]
    k_refs = refs[1:1 + n_kv]
    vt_refs = refs[1 + n_kv:1 + 2 * n_kv]
    lam_ref, gnw_ref, zero_ref, o_ref, sbuf0, sbuf1, ebuf0, ebuf1 = refs[1 + 2 * n_kv:]
    sbufs, ebufs = (sbuf0, sbuf1), (ebuf0, ebuf1)
    sub = DIFF_SUB
    seg = [0]
    for k_ref in k_refs:
        seg.append(seg[-1] + k_ref.shape[0])
    keys = seg[-1]
    base = pl.multiple_of(zero_ref[0], LANES)
    lp = lam_ref[...]
    lam = (jnp.exp(jnp.sum(lp[0:1] * lp[1:2], axis=-1, keepdims=True))
           - jnp.exp(jnp.sum(lp[2:3] * lp[3:4], axis=-1, keepdims=True)) + lam_init)
    lane = lax.broadcasted_iota(jnp.int32, (sub, 2 * LANES), 1)
    grp = (lane % LANES) // 32
    items = [(qs, h) for qs in range(n_sub) for h in range(DIFF_HEADS)]

    def scores(idx, slot):
        qs, h = items[idx]
        hp, hl = divmod(h, 2)
        cols = slice(2 * LANES * hp, 2 * LANES * (hp + 1))
        qh = q_ref[qs * sub:(qs + 1) * sub, cols]
        q2 = jnp.concatenate([jnp.where(grp == 2 * hl + m, qh, jnp.zeros_like(qh))
                              for m in range(2)], axis=0)
        mx = None
        for j, k_ref in enumerate(k_refs):
            st = lax.dot_general(k_ref[:, cols], q2, (((1,), (1,)), ((), ())),
                                 preferred_element_type=F32)
            sbufs[slot][pl.ds(base + seg[j], seg[j + 1] - seg[j]), :] = st
            mj = jnp.max(st, axis=0, keepdims=True)
            mx = mj if mx is None else jnp.maximum(mx, mj)
        return mx

    def weights(slot, mx):
        for r0 in range(0, keys, DIFF_KB):
            rows = pl.ds(base + r0, DIFF_KB)
            x = (sbufs[slot][rows, :] - mx).astype(BF16)
            ebufs[slot][rows, :] = jnp.exp2(x)

    def output(idx, slot):
        qs, h = items[idx]
        ot = None
        for j, vt_ref in enumerate(vt_refs):
            n_j = seg[j + 1] - seg[j]
            va = jnp.concatenate([vt_ref[DIFF_DV * h:DIFF_DV * (h + 1), :],
                                  jnp.ones((ONES_ROWS, n_j), BF16)], axis=0)
            oj = jnp.dot(va, ebufs[slot][pl.ds(base + seg[j], n_j), :], preferred_element_type=F32)
            ot = oj if ot is None else ot + oj
        l = ot[DIFF_DV:DIFF_DV + 1, :]
        d = ot[:DIFF_DV, :sub] / l[:, :sub] - lam * (ot[:DIFF_DV, sub:] / l[:, sub:])
        ms = jnp.mean(d * d, axis=0, keepdims=True)
        y = (d * lax.rsqrt(ms + EPS)).T
        hcols = slice(DIFF_DV * h, DIFF_DV * (h + 1))
        o_ref[qs * sub:(qs + 1) * sub, hcols] = (y * gnw_ref[:, hcols] * (1.0 - lam_init)).astype(BF16)

    n = len(items)
    mxs = {0: scores(0, 0)}
    if n > 1:
        mxs[1] = scores(1, 1)
    weights(0, mxs.pop(0))
    for idx in range(n):
        if idx + 2 < n:
            mxs[idx + 2] = scores(idx + 2, idx % 2)
        if idx + 1 < n:
            weights((idx + 1) % 2, mxs.pop(idx + 1))
        output(idx, idx % 2)


def _diff_attention(p_q, p_kv, vt_kv, diff_lam, gn_w, lam_init, qt):
    w = DIFF_QK
    rows = p_q.shape[1]
    n_kv = len(p_kv)
    keys = sum(a.shape[1] for a in p_kv)
    return pl.pallas_call(
        functools.partial(_diff_kernel, lam_init=lam_init, n_sub=qt // DIFF_SUB, n_kv=n_kv),
        grid=(BATCH, rows // qt),
        in_specs=[pl.BlockSpec((None, qt, w), lambda b, t: (b, t, OFF_DQ // w))]
                 + [pl.BlockSpec((None, a.shape[1], w), lambda b, t: (b, 0, OFF_DK // w)) for a in p_kv]
                 + [pl.BlockSpec((None, DIFF_WIDTH, a.shape[2]), lambda b, t: (b, 0, 0)) for a in vt_kv]
                 + [pl.BlockSpec((4, DIFF_DK), lambda b, t: (0, 0)), pl.BlockSpec((1, w), lambda b, t: (0, 0)),
                    pl.BlockSpec(memory_space=pltpu.SMEM)],
        out_specs=pl.BlockSpec((None, qt, w), lambda b, t: (b, t, 0)),
        out_shape=jax.ShapeDtypeStruct((BATCH, rows, w), BF16),
        scratch_shapes=[pltpu.VMEM((keys, 2 * DIFF_SUB), F32)] * 2
                       + [pltpu.VMEM((keys, 2 * DIFF_SUB), BF16)] * 2,
        compiler_params=_params(2),
        name="diff_attn" if n_kv > 1 else "diff_attn_ctx",
    )(p_q, *p_kv, *vt_kv, diff_lam, gn_w, jnp.zeros((1,), jnp.int32))


CONV_RES = 2 * SUBLANES
CONV_GAP = CONV_RES
CONV_TILE = 128
CONV_WIN = CONV_TILE + CONV_RES


def _conv_rows(n):
    return CONV_GAP + n + CONV_GAP


def _conv_kernel(*refs, n_seq):
    ins = [refs[2 * i:2 * i + 2] for i in range(n_seq)]
    w_ref, b_ref, lnw_ref, lnb_ref = refs[2 * n_seq:2 * n_seq + 4]
    outs = refs[2 * n_seq + 4:3 * n_seq + 4]
    ubufs = refs[3 * n_seq + 4:4 * n_seq + 4]
    zbuf = refs[4 * n_seq + 4]
    ch = CONV_CH
    wb = w_ref[...].astype(BF16)

    kcol = lax.broadcasted_iota(jnp.int32, (CONV_TILE, CONV_RES * CONV_WIN), 1)
    krow = lax.broadcasted_iota(jnp.int32, (CONV_TILE, CONV_RES * CONV_WIN), 0)
    sel = jnp.where((kcol % CONV_WIN) == krow + kcol // CONV_WIN, 1.0, 0.0).astype(BF16)

    for (cv_ref, cg_ref), o_ref, ubuf in zip(ins, outs, ubufs):
        n = cv_ref.shape[0]
        ubuf[0:CONV_GAP, :] = jnp.zeros((CONV_GAP, ch), BF16)
        ubuf[CONV_GAP + n:_conv_rows(n), :] = jnp.zeros((CONV_GAP, ch), BF16)
        ubuf[CONV_GAP:CONV_GAP + n, :] = (cv_ref[...].astype(F32)
                                          * _sigmoid(cg_ref[...].astype(F32))).astype(BF16)
        for r0 in range(0, n, CONV_TILE):
            t0 = CONV_GAP + r0
            for r in range(CONV_RES):
                z = ubuf[t0:t0 + CONV_WIN, :] * wb[r + CONV_PAD:r + CONV_PAD + 1, :]
                if r >= 1:
                    z = z + (ubuf[t0 - CONV_RES:t0 - CONV_RES + CONV_WIN, :]
                             * wb[r - CONV_RES + CONV_PAD:r - CONV_RES + CONV_PAD + 1, :])
                zbuf[CONV_WIN * r:CONV_WIN * (r + 1), :] = z
            y = jnp.dot(sel, zbuf[...], preferred_element_type=F32) + b_ref[...]
            mu = jnp.mean(y, axis=-1, keepdims=True)
            dev = y - mu
            var = jnp.mean(dev * dev, axis=-1, keepdims=True)
            zn = dev * lax.rsqrt(var + EPS) * lnw_ref[...] + lnb_ref[...]
            o_ref[r0:r0 + CONV_TILE, :] = _silu(zn).astype(BF16)


N_RET_IN = 10
N_RET_SCRATCH = 6


def _ret_conv_kernel(*refs, need_ctx):
    n_seq = 2 if need_ctx else 1
    n_in = N_RET_IN + 2 * n_seq + 4
    ret_in, conv_in = refs[:N_RET_IN], refs[N_RET_IN:n_in]
    outs, scratch = refs[n_in:n_in + 2 * n_seq], refs[n_in + 2 * n_seq:]
    _ret_kernel(*ret_in, *outs[:n_seq], *scratch[:N_RET_SCRATCH], need_ctx=need_ctx)
    _conv_kernel(*conv_in, *outs[n_seq:], *scratch[N_RET_SCRATCH:], n_seq=n_seq)


def _ret_conv(p_lat, p_ctx, ret_log_decay, gn_w, conv_w, conv_b, ln_w, ln_b, need_ctx):
    w, ch = RET_QK, CONV_CH
    lat = lambda j, width: pl.BlockSpec((None, SEQ, width), lambda b: (b, 0, j))
    ctx = lambda j, width: pl.BlockSpec((None, CTX_LEN, width), lambda b: (b, 0, j))
    vec = lambda width: pl.BlockSpec((1, width), lambda b: (0, 0))
    ret_cols = (OFF_RQ // w, OFF_RK // w, OFF_RV // w, OFF_RG // w)
    seqs = [(lat, SEQ, p_lat)] + ([(ctx, CTX_LEN, p_ctx)] if need_ctx else [])
    in_specs = ([pl.BlockSpec(memory_space=pltpu.SMEM)]
                + [lat(j, w) for j in ret_cols] + [ctx(j, w) for j in ret_cols] + [vec(w)])
    args = [ret_log_decay] + [p_lat] * 4 + [p_ctx] * 4 + [gn_w]
    for spec, _, p in seqs:
        in_specs += [spec(OFF_CV // ch, ch), spec(OFF_CG // ch, ch)]
        args += [p, p]
    in_specs += [pl.BlockSpec((CONV_WIDTH, ch), lambda b: (0, 0)), vec(ch), vec(ch), vec(ch)]
    args += [conv_w, conv_b, ln_w, ln_b]
    outs = pl.pallas_call(
        functools.partial(_ret_conv_kernel, need_ctx=need_ctx),
        grid=(BATCH,),
        in_specs=in_specs,
        out_specs=[spec(0, w) for spec, _, _ in seqs] + [spec(0, ch) for spec, _, _ in seqs],
        out_shape=[jax.ShapeDtypeStruct((BATCH, n, w), BF16) for _, n, _ in seqs]
                  + [jax.ShapeDtypeStruct((BATCH, n, ch), BF16) for _, n, _ in seqs],
        scratch_shapes=[
            pltpu.VMEM((RET_HEADS * RET_C, RET_C), F32),
            pltpu.VMEM((6, RET_C, w), F32),
            pltpu.VMEM((2, RET_NC, w, w), F32),
            pltpu.VMEM((2, RET_NC, w, w), BF16),
            pltpu.VMEM((RET_NC, RET_HEADS * RET_C, RET_C), BF16),
            pltpu.VMEM((RET_NC, RET_C, w), F32),
        ] + [pltpu.VMEM((_conv_rows(n), ch), BF16) for _, n, _ in seqs]
          + [pltpu.VMEM((CONV_RES * CONV_WIN, ch), BF16)],
        compiler_params=_params(1),
        name="retention_conv",
    )(*args)
    if need_ctx:
        return outs[0], outs[1], outs[2], outs[3]
    return outs[0], None, outs[1], None


def _out_mlp_kernel(ret_ref, dif_ref, cnv_ref, x_ref, mod_ref, nw_ref, wo_ref, w1_ref, w2_ref,
                    fin_ref, o_ref, *, final):
    d = D_MODEL
    y = jnp.dot(ret_ref[...], wo_ref[0:RET_WIDTH, :], preferred_element_type=F32)
    y = y + jnp.dot(dif_ref[...], wo_ref[RET_WIDTH:RET_WIDTH + DIFF_WIDTH, :],
                    preferred_element_type=F32)
    y = y + jnp.dot(cnv_ref[...], wo_ref[RET_WIDTH + DIFF_WIDTH:MIX_WIDTH, :],
                    preferred_element_type=F32)
    x1 = x_ref[...] + mod_ref[:, 2 * d:3 * d] * y
    h2 = _rmsnorm_mod(x1, nw_ref[...], mod_ref[:, 3 * d:4 * d], mod_ref[:, 4 * d:5 * d]).astype(BF16)
    u = jnp.maximum(jnp.dot(h2, w1_ref[...], preferred_element_type=F32), 0.0)
    z = jnp.dot((u * u).astype(BF16), w2_ref[...], preferred_element_type=F32)
    x2 = x1 + mod_ref[:, 5 * d:6 * d] * z
    if final:
        ms = jnp.mean(x2 * x2, axis=-1, keepdims=True)
        x2 = x2 * lax.rsqrt(ms + EPS) * fin_ref[...]
    o_ref[...] = x2


def _out_mlp(ret, dif, cnv, xs, mod, nw, wo, w1, w2, layer, fin, tm, is_ctx, final):
    rows = xs.shape[1]
    const = lambda shape: pl.BlockSpec(shape, lambda b, t: (0,) * len(shape),
                                       pipeline_mode=pl.Buffered(1))
    weight = lambda shape: pl.BlockSpec((None,) + shape, lambda b, t: (layer, 0, 0),
                                        pipeline_mode=pl.Buffered(1))
    tile = lambda width: pl.BlockSpec((None, tm, width), lambda b, t: (b, t, 0))
    mod_row = (lambda b, t: (BATCH, 0, 0)) if is_ctx else (lambda b, t: (b, 0, 0))
    return pl.pallas_call(
        functools.partial(_out_mlp_kernel, final=final),
        grid=(BATCH, rows // tm),
        in_specs=[
            tile(RET_WIDTH), tile(DIFF_WIDTH), tile(CONV_CH), tile(D_MODEL),
            pl.BlockSpec((None, 1, N_MOD * D_MODEL), mod_row),
            const((1, D_MODEL)),
            weight((MIX_WIDTH, D_MODEL)), weight((D_MODEL, D_FF)), weight((D_FF, D_MODEL)),
            const((1, D_MODEL)),
        ],
        out_specs=tile(D_MODEL),
        out_shape=jax.ShapeDtypeStruct((BATCH, rows, D_MODEL), F32),
        compiler_params=_params(2),
        name="out_mlp_ctx" if is_ctx else "out_mlp",
    )(ret, dif, cnv, xs, mod, nw, wo, w1, w2, fin)


def kernel(x, c, ctx, c_ctx, norm1_w, norm2_w, ada_w, ada_b, w_in, ret_log_decay, ret_gn_w, diff_lam, diff_gn_w, conv_w, conv_b, conv_ln_w, conv_ln_b, w_out, mlp_w1, mlp_w2, final_norm_w):
    assert x.shape == (BATCH, SEQ, D_MODEL) and ctx.shape == (BATCH, CTX_LEN, D_MODEL)
    tables = _rope_tables()
    cc = jnp.concatenate([c, c_ctx[None, :], jnp.zeros((MOD_ROWS - BATCH - 1, D_MODEL), F32)], axis=0)
    mod_all = _ada(cc, ada_w, ada_b).reshape(DEPTH, MOD_ROWS, 1, N_MOD * D_MODEL)
    fin = final_norm_w.reshape(1, D_MODEL)
    w_bf = _reorder_in_proj(w_in).astype(BF16)
    wo, w1, w2 = w_out.astype(BF16), mlp_w1.astype(BF16), mlp_w2.astype(BF16)
    for i in range(DEPTH):
        last = i == DEPTH - 1
        lam_init = 0.8 - 0.6 * math.exp(-0.3 * i)
        mod = mod_all[i]
        nw1 = norm1_w[i].reshape(1, D_MODEL)
        nw2 = norm2_w[i].reshape(1, D_MODEL)
        dgn = diff_gn_w[i].reshape(1, DIFF_WIDTH)
        p_lat, vt_lat = _inproj(x, mod, nw1, w_bf, i, tables, TM_LAT)
        p_ctx, vt_ctx = _inproj(ctx, mod, nw1, w_bf, i, None, TM_CTX)
        ret_lat, ret_ctx, cnv_lat, cnv_ctx = _ret_conv(
            p_lat, p_ctx, ret_log_decay[i], ret_gn_w[i].reshape(1, RET_WIDTH),
            conv_w[i], conv_b[i].reshape(1, CONV_CH), conv_ln_w[i].reshape(1, CONV_CH),
            conv_ln_b[i].reshape(1, CONV_CH), not last)
        dif_lat = _diff_attention(p_lat, [p_lat, p_ctx], [vt_lat, vt_ctx], diff_lam[i], dgn, lam_init,
                                  DIFF_QT)
        if not last:
            dif_ctx = _diff_attention(p_ctx, [p_ctx], [vt_ctx], diff_lam[i], dgn, lam_init, CTX_LEN)
            ctx = _out_mlp(ret_ctx, dif_ctx, cnv_ctx, ctx, mod, nw2, wo, w1, w2, i, fin, TM_CTX, True, False)
        x = _out_mlp(ret_lat, dif_lat, cnv_lat, x, mod, nw2, wo, w1, w2, i, fin, TM_LAT, False, last)
    return x
```

```python
import functools
import math

import jax
import jax.numpy as jnp
from jax import lax
from jax.experimental import pallas as pl
from jax.experimental.pallas import tpu as pltpu

D_MODEL = 1024
BATCH = 16
SEQ = 2048
DEPTH = 2
CTX_LEN = 256
GRID_W = 64
EPS = 1e-6
ROPE_BASE = 10000.0
RET_HEADS = 4
RET_DK = 64
RET_DV = 64
RET_QK = RET_HEADS * RET_DK
RET_WIDTH = RET_HEADS * RET_DV
DIFF_HEADS = 4
DIFF_DK = 64
DIFF_DV = 2 * DIFF_DK
DIFF_QK = DIFF_HEADS * 2 * DIFF_DK
DIFF_WIDTH = DIFF_HEADS * DIFF_DV
CONV_CH = D_MODEL // 4
CONV_WIDTH = 31
CONV_PAD = (CONV_WIDTH - 1) // 2
MIX_WIDTH = RET_WIDTH + DIFF_WIDTH + CONV_CH
IN_WIDTH = 2 * RET_QK + 2 * RET_WIDTH + 2 * DIFF_QK + DIFF_WIDTH + 2 * CONV_CH
D_FF = 4 * D_MODEL
N_MOD = 6

TOK = SEQ + CTX_LEN
LANES = 128
SUBLANES = 8
TM_LAT = 512
TM_CTX = CTX_LEN
MOD_ROWS = 24
VMEM_LIMIT = 56 * 1024 * 1024

OFF_RQ = 0
OFF_RK = OFF_RQ + RET_QK
OFF_RV = OFF_RK + RET_QK
OFF_RG = OFF_RV + RET_WIDTH
OFF_DQ = OFF_RG + RET_WIDTH
OFF_DK = OFF_DQ + DIFF_QK
OFF_CV = OFF_DK + DIFF_QK
OFF_CG = OFF_CV + CONV_CH
P_WIDTH = OFF_CG + CONV_CH
OFF_DV = P_WIDTH
SRC_DV = OFF_DK + DIFF_QK
SRC_CV = SRC_DV + DIFF_WIDTH

F32 = jnp.float32
BF16 = jnp.bfloat16
LOG2_E = math.log2(math.e)


def _params(n_axes):
    return pltpu.CompilerParams(dimension_semantics=("arbitrary",) * n_axes,
                                vmem_limit_bytes=VMEM_LIMIT)


def _sigmoid(x):
    return 1.0 / (1.0 + jnp.exp(-x))


def _silu(x):
    return x * _sigmoid(x)


def _reorder_in_proj(w):
    lead = w.shape[:2]
    rq, rk, rvg, dq, dk, dv, cvg = jnp.split(
        w, [OFF_RK, OFF_RV, OFF_DQ, OFF_DK, SRC_DV, SRC_CV], axis=2)

    def ret(a):
        half = RET_DK // 2
        return a.reshape(*lead, RET_HEADS, 2, half).transpose(0, 1, 3, 2, 4).reshape(*lead, RET_QK)

    def dif(a):
        q = DIFF_DK // 4
        return (a.reshape(*lead, DIFF_HEADS // 2, 2, 2, 2, 2, q).transpose(0, 1, 2, 6, 3, 4, 5, 7)
                .reshape(*lead, DIFF_QK))

    return jnp.concatenate([ret(rq), ret(rk), rvg, dif(dq), dif(dk), cvg, dv], axis=2)


def _rope_tables():
    def angles(pos, dim):
        inv = ROPE_BASE ** (-jnp.arange(0, dim, 2, dtype=F32) / dim)
        return pos.astype(F32)[:, None] * inv[None, :]

    t = jnp.arange(SEQ)
    ang_r = jnp.tile(angles(t, RET_DK), (1, RET_HEADS))
    half = DIFF_DK // 2
    ang_d = jnp.concatenate([angles(t // GRID_W, half), angles(t % GRID_W, half)], axis=1)
    ang_d = jnp.tile(ang_d, (1, 4))
    return jnp.cos(ang_r), jnp.sin(ang_r), jnp.cos(ang_d), jnp.sin(ang_d)


ADA_TN = 1536


def _ada_kernel(cc_ref, w_ref, b_ref, o_ref):
    s = _silu(cc_ref[...]).astype(BF16)
    o_ref[...] = jnp.dot(s, w_ref[...].astype(BF16), preferred_element_type=F32) + b_ref[...]


def _ada(cc, ada_w, ada_b):
    n = N_MOD * D_MODEL
    return pl.pallas_call(
        _ada_kernel,
        grid=(DEPTH, n // ADA_TN),
        in_specs=[
            pl.BlockSpec((MOD_ROWS, D_MODEL), lambda i, j: (0, 0)),
            pl.BlockSpec((None, D_MODEL, ADA_TN), lambda i, j: (i, 0, j)),
            pl.BlockSpec((None, 1, ADA_TN), lambda i, j: (i, 0, j)),
        ],
        out_specs=pl.BlockSpec((None, MOD_ROWS, ADA_TN), lambda i, j: (i, 0, j)),
        out_shape=jax.ShapeDtypeStruct((DEPTH, MOD_ROWS, n), F32),
        compiler_params=_params(2),
        name="ada_mod",
    )(cc, ada_w, ada_b.reshape(DEPTH, 1, n))


def _rmsnorm_mod(x, nw, shift, scale):
    ms = jnp.mean(x * x, axis=-1, keepdims=True)
    y = x * lax.rsqrt(ms + EPS) * nw
    return y * (1.0 + scale) + shift


def _inproj_kernel(*refs, rope):
    if rope:
        x_ref, mod_ref, nw_ref, w_ref, cr_ref, sr_ref, cd_ref, sd_ref, p_ref, vt_ref = refs
    else:
        x_ref, mod_ref, nw_ref, w_ref, p_ref, vt_ref = refs
    d = D_MODEL
    h = _rmsnorm_mod(x_ref[...], nw_ref[...], mod_ref[:, 0:d], mod_ref[:, d:2 * d]).astype(BF16)

    def proj(off, width):
        return jnp.dot(h, w_ref[:, off:off + width], preferred_element_type=F32)

    def rope_store(off, cos, sin, mult):
        z = proj(off, 2 * LANES)
        if rope:
            za, zb = z[:, :LANES], z[:, LANES:]
            p_ref[:, off:off + LANES] = ((za * cos - zb * sin) * mult).astype(BF16)
            p_ref[:, off + LANES:off + 2 * LANES] = ((zb * cos + za * sin) * mult).astype(BF16)
        else:
            p_ref[:, off:off + 2 * LANES] = (z * mult).astype(BF16)

    if rope:
        cr, sr, cd, sd = cr_ref[...], sr_ref[...], cd_ref[...], sd_ref[...]
    else:
        cr = sr = cd = sd = None
    rope_store(OFF_RQ, cr, sr, 1.0)
    rope_store(OFF_RK, cr, sr, RET_DK ** -0.5)
    for g in range(DIFF_QK // (2 * LANES)):
        rope_store(OFF_DQ + 2 * LANES * g, cd, sd, DIFF_DK ** -0.5 * LOG2_E)
        rope_store(OFF_DK + 2 * LANES * g, cd, sd, 1.0)
    for off, width in ((OFF_RV, 2 * RET_WIDTH), (OFF_CV, 2 * CONV_CH)):
        p_ref[:, off:off + width] = proj(off, width).astype(BF16)
    vt_ref[...] = proj(OFF_DV, DIFF_WIDTH).T.astype(BF16)


def _inproj(xs, mod, nw, w_bf, layer, tables, tm):
    rows = xs.shape[1]
    rope = tables is not None
    const = lambda shape: pl.BlockSpec(shape, lambda t, b: (0,) * len(shape))
    mod_row = (lambda t, b: (b, 0, 0)) if rope else (lambda t, b: (BATCH, 0, 0))
    in_specs = [
        pl.BlockSpec((None, tm, D_MODEL), lambda t, b: (b, t, 0)),
        pl.BlockSpec((None, 1, N_MOD * D_MODEL), mod_row),
        const((1, D_MODEL)),
        pl.BlockSpec((None, D_MODEL, IN_WIDTH), lambda t, b: (layer, 0, 0)),
    ]
    args = [xs, mod, nw, w_bf]
    if rope:
        in_specs += [pl.BlockSpec((tm, LANES), lambda t, b: (t, 0))] * 4
        args += list(tables)
    return pl.pallas_call(
        functools.partial(_inproj_kernel, rope=rope),
        grid=(rows // tm, BATCH),
        in_specs=in_specs,
        out_specs=[pl.BlockSpec((None, tm, P_WIDTH), lambda t, b: (b, t, 0)),
                   pl.BlockSpec((None, DIFF_WIDTH, tm), lambda t, b: (b, 0, t))],
        out_shape=[jax.ShapeDtypeStruct((BATCH, rows, P_WIDTH), BF16),
                   jax.ShapeDtypeStruct((BATCH, DIFF_WIDTH, rows), BF16)],
        compiler_params=_params(2),
        name="in_proj" if rope else "in_proj_ctx",
    )(*args)


RET_C = 256
RET_LAT_NC = SEQ // RET_C
RET_NC = RET_LAT_NC + CTX_LEN // RET_C
QD_F, QD_B, KD_F, KD_B, CD_F, CD_B = range(6)


def _ret_kernel(*refs, need_ctx):
    ld_ref, ql, kl, vl, gl, qc_, kc_, vc_, gc_, gnw_ref = refs[:10]
    if need_ctx:
        ol, oc, dtab, dvec, kvbuf, sbuf, pwbuf, obuf = refs[10:]
    else:
        ol, dtab, dvec, kvbuf, sbuf, pwbuf, obuf = refs[10:]
        oc = None
    c = RET_C
    w = RET_QK
    assert c == w

    def chunk(n):
        if n < RET_LAT_NC:
            return ql, kl, vl, gl, ol, slice(n * c, (n + 1) * c)
        return qc_, kc_, vc_, gc_, oc, slice(0, c)

    lane = lax.broadcasted_iota(jnp.int32, (c, w), 1)
    rowi = lax.broadcasted_iota(jnp.int32, (c, w), 0)
    row = rowi.astype(F32)
    hq = (lane % LANES) // 32
    hv = lane // RET_DV

    @pl.when(pl.program_id(0) == 0)
    def _():
        rel = rowi - lane
        rel_f = jnp.maximum(rel, 0).astype(F32)
        rel_b = jnp.maximum(-rel, 0).astype(F32)
        lgq = [jnp.zeros((c, w), F32), jnp.zeros((c, w), F32)]
        lgv = [jnp.zeros((c, w), F32), jnp.zeros((c, w), F32)]
        for h in range(RET_HEADS):
            lg = [-jnp.exp(jnp.full((c, w), ld_ref[d, h], F32)) for d in range(2)]
            dtab[h * c:(h + 1) * c, :] = jnp.where(
                rel > 0, jnp.exp(lg[0] * rel_f), jnp.where(rel < 0, jnp.exp(lg[1] * rel_b), 2.0))
            for d in range(2):
                lgq[d] = jnp.where(hq == h, lg[d], lgq[d])
                lgv[d] = jnp.where(hv == h, lg[d], lgv[d])
        dvec[QD_F] = jnp.exp(lgq[0] * (row + 1.0))
        dvec[QD_B] = jnp.exp(lgq[1] * (c - row))
        dvec[KD_F] = jnp.exp(lgq[0] * (c - 1.0 - row))
        dvec[KD_B] = jnp.exp(lgq[1] * row)
        dvec[CD_F] = jnp.exp(lgv[0] * float(c))
        dvec[CD_B] = jnp.exp(lgv[1] * float(c))

    same_head = ((rowi % LANES) // 32) == (lane // RET_DV)
    group_avg = jnp.where((rowi // RET_DV) == (lane // RET_DV), 1.0 / RET_DV, 0.0).astype(BF16)

    def hi_lo_dot(a, m):
        hi = a.astype(BF16)
        lo = (a - hi.astype(F32)).astype(BF16)
        return (jnp.dot(hi, m, preferred_element_type=F32)
                + jnp.dot(lo, m, preferred_element_type=F32))

    for n in range(RET_NC):
        _, k_ref, v_ref, _, _, rows = chunk(n)
        kc = k_ref[rows, :].astype(F32)
        k2 = jnp.concatenate([(kc * dvec[KD_F]).astype(BF16), (kc * dvec[KD_B]).astype(BF16)], axis=1)
        kv = lax.dot_general(k2, v_ref[rows, :], (((0,), (0,)), ((), ())),
                             preferred_element_type=F32)
        kvbuf[0, n] = jnp.where(same_head, kv[:w], 0.0)
        kvbuf[1, n] = jnp.where(same_head, kv[w:], 0.0)

    orders = ([RET_LAT_NC] + list(range(RET_LAT_NC)), [RET_LAT_NC] + list(range(RET_LAT_NC - 1, -1, -1)))
    for d, order in enumerate(orders):
        s = jnp.zeros((w, w), F32)
        for n in order[:-1]:
            sbuf[d, n] = s.astype(BF16)
            s = s * dvec[CD_F + d] + kvbuf[d, n]
        sbuf[d, order[-1]] = s.astype(BF16)

    out_chunks = range(RET_NC if need_ctx else RET_LAT_NC)
    for n in out_chunks:
        q_ref, k_ref, _, _, _, rows = chunk(n)
        qc = q_ref[rows, :]
        q4 = jnp.concatenate([jnp.where(hq == h, qc, jnp.zeros_like(qc)) for h in range(RET_HEADS)],
                             axis=0)
        s = lax.dot_general(q4, k_ref[rows, :], (((1,), (1,)), ((), ())),
                            preferred_element_type=F32)
        pwbuf[n] = (s * dtab[...]).astype(BF16)
    for n in out_chunks:
        q_ref, _, v_ref, _, _, rows = chunk(n)
        vc = v_ref[rows, :]
        qf = q_ref[rows, :].astype(F32)
        lhs = jnp.concatenate([pwbuf[n, h * c:(h + 1) * c] for h in range(RET_HEADS)]
                              + [(qf * dvec[QD_F]).astype(BF16), (qf * dvec[QD_B]).astype(BF16)], axis=1)
        rhs = jnp.concatenate([jnp.where(hv == h, vc, jnp.zeros_like(vc)) for h in range(RET_HEADS)]
                              + [sbuf[0, n], sbuf[1, n]], axis=0)
        obuf[n] = jnp.dot(lhs, rhs, preferred_element_type=F32)
    for n in out_chunks:
        _, _, _, g_ref, o_ref, rows = chunk(n)
        o = obuf[n]
        mu = hi_lo_dot(o, group_avg)
        dev = o - mu
        var = hi_lo_dot(dev * dev, group_avg)
        t = dev * lax.rsqrt(var + EPS) * gnw_ref[...]
        g = g_ref[rows, :].astype(F32)
        o_ref[rows, :] = (t * _silu(g)).astype(BF16)


DIFF_QT = 256
DIFF_SUB = 256
DIFF_KB = 64
ONES_ROWS = 16


def _diff_kernel(*refs, lam_init, n_sub, n_kv):
    q_ref = refs[0]
    k_refs = refs[1:1 + n_kv]
    vt_refs = refs[1 + n_kv:1 + 2 * n_kv]
    lam_ref, gnw_ref, zero_ref, o_ref, sbuf0, sbuf1, ebuf0, ebuf1 = refs[1 + 2 * n_kv:]
    sbufs, ebufs = (sbuf0, sbuf1), (ebuf0, ebuf1)
    sub = DIFF_SUB
    seg = [0]
    for k_ref in k_refs:
        seg.append(seg[-1] + k_ref.shape[0])
    keys = seg[-1]
    base = pl.multiple_of(zero_ref[0], LANES)
    lp = lam_ref[...]
    lam = (jnp.exp(jnp.sum(lp[0:1] * lp[1:2], axis=-1, keepdims=True))
           - jnp.exp(jnp.sum(lp[2:3] * lp[3:4], axis=-1, keepdims=True)) + lam_init)
    lane = lax.broadcasted_iota(jnp.int32, (sub, 2 * LANES), 1)
    grp = (lane % LANES) // 32
    items = [(qs, h) for qs in range(n_sub) for h in range(DIFF_HEADS)]

    def scores(idx, slot):
        qs, h = items[idx]
        hp, hl = divmod(h, 2)
        cols = slice(2 * LANES * hp, 2 * LANES * (hp + 1))
        qh = q_ref[qs * sub:(qs + 1) * sub, cols]
        q2 = jnp.concatenate([jnp.where(grp == 2 * hl + m, qh, jnp.zeros_like(qh))
                              for m in range(2)], axis=0)
        mx = None
        for j, k_ref in enumerate(k_refs):
            st = lax.dot_general(k_ref[:, cols], q2, (((1,), (1,)), ((), ())),
                                 preferred_element_type=F32)
            sbufs[slot][pl.ds(base + seg[j], seg[j + 1] - seg[j]), :] = st
            mj = jnp.max(st, axis=0, keepdims=True)
            mx = mj if mx is None else jnp.maximum(mx, mj)
        return mx

    def weights(slot, mx):
        for r0 in range(0, keys, DIFF_KB):
            rows = pl.ds(base + r0, DIFF_KB)
            x = (sbufs[slot][rows, :] - mx).astype(BF16)
            ebufs[slot][rows, :] = jnp.exp2(x)

    def output(idx, slot):
        qs, h = items[idx]
        ot = None
        for j, vt_ref in enumerate(vt_refs):
            n_j = seg[j + 1] - seg[j]
            va = jnp.concatenate([vt_ref[DIFF_DV * h:DIFF_DV * (h + 1), :],
                                  jnp.ones((ONES_ROWS, n_j), BF16)], axis=0)
            oj = jnp.dot(va, ebufs[slot][pl.ds(base + seg[j], n_j), :], preferred_element_type=F32)
            ot = oj if ot is None else ot + oj
        l = ot[DIFF_DV:DIFF_DV + 1, :]
        d = ot[:DIFF_DV, :sub] / l[:, :sub] - lam * (ot[:DIFF_DV, sub:] / l[:, sub:])
        ms = jnp.mean(d * d, axis=0, keepdims=True)
        y = (d * lax.rsqrt(ms + EPS)).T
        hcols = slice(DIFF_DV * h, DIFF_DV * (h + 1))
        o_ref[qs * sub:(qs + 1) * sub, hcols] = (y * gnw_ref[:, hcols] * (1.0 - lam_init)).astype(BF16)

    n = len(items)
    mxs = {0: scores(0, 0)}
    if n > 1:
        mxs[1] = scores(1, 1)
    weights(0, mxs.pop(0))
    for idx in range(n):
        if idx + 2 < n:
            mxs[idx + 2] = scores(idx + 2, idx % 2)
        if idx + 1 < n:
            weights((idx + 1) % 2, mxs.pop(idx + 1))
        output(idx, idx % 2)


def _diff_attention(p_q, p_kv, vt_kv, diff_lam, gn_w, lam_init, qt):
    w = DIFF_QK
    rows = p_q.shape[1]
    n_kv = len(p_kv)
    keys = sum(a.shape[1] for a in p_kv)
    return pl.pallas_call(
        functools.partial(_diff_kernel, lam_init=lam_init, n_sub=qt // DIFF_SUB, n_kv=n_kv),
        grid=(BATCH, rows // qt),
        in_specs=[pl.BlockSpec((None, qt, w), lambda b, t: (b, t, OFF_DQ // w))]
                 + [pl.BlockSpec((None, a.shape[1], w), lambda b, t: (b, 0, OFF_DK // w)) for a in p_kv]
                 + [pl.BlockSpec((None, DIFF_WIDTH, a.shape[2]), lambda b, t: (b, 0, 0)) for a in vt_kv]
                 + [pl.BlockSpec((4, DIFF_DK), lambda b, t: (0, 0)), pl.BlockSpec((1, w), lambda b, t: (0, 0)),
                    pl.BlockSpec(memory_space=pltpu.SMEM)],
        out_specs=pl.BlockSpec((None, qt, w), lambda b, t: (b, t, 0)),
        out_shape=jax.ShapeDtypeStruct((BATCH, rows, w), BF16),
        scratch_shapes=[pltpu.VMEM((keys, 2 * DIFF_SUB), F32)] * 2
                       + [pltpu.VMEM((keys, 2 * DIFF_SUB), BF16)] * 2,
        compiler_params=_params(2),
        name="diff_attn" if n_kv > 1 else "diff_attn_ctx",
    )(p_q, *p_kv, *vt_kv, diff_lam, gn_w, jnp.zeros((1,), jnp.int32))


CONV_RES = 2 * SUBLANES
CONV_GAP = CONV_RES
CONV_TILE = 128
CONV_WIN = CONV_TILE + CONV_RES


def _conv_rows(n):
    return CONV_GAP + n + CONV_GAP


def _conv_kernel(*refs, n_seq):
    ins = [refs[2 * i:2 * i + 2] for i in range(n_seq)]
    w_ref, b_ref, lnw_ref, lnb_ref = refs[2 * n_seq:2 * n_seq + 4]
    outs = refs[2 * n_seq + 4:3 * n_seq + 4]
    ubufs = refs[3 * n_seq + 4:4 * n_seq + 4]
    zbuf = refs[4 * n_seq + 4]
    ch = CONV_CH
    wb = w_ref[...].astype(BF16)

    kcol = lax.broadcasted_iota(jnp.int32, (CONV_TILE, CONV_RES * CONV_WIN), 1)
    krow = lax.broadcasted_iota(jnp.int32, (CONV_TILE, CONV_RES * CONV_WIN), 0)
    sel = jnp.where((kcol % CONV_WIN) == krow + kcol // CONV_WIN, 1.0, 0.0).astype(BF16)

    for (cv_ref, cg_ref), o_ref, ubuf in zip(ins, outs, ubufs):
        n = cv_ref.shape[0]
        ubuf[0:CONV_GAP, :] = jnp.zeros((CONV_GAP, ch), BF16)
        ubuf[CONV_GAP + n:_conv_rows(n), :] = jnp.zeros((CONV_GAP, ch), BF16)
        ubuf[CONV_GAP:CONV_GAP + n, :] = (cv_ref[...].astype(F32)
                                          * _sigmoid(cg_ref[...].astype(F32))).astype(BF16)
        for r0 in range(0, n, CONV_TILE):
            t0 = CONV_GAP + r0
            for r in range(CONV_RES):
                z = ubuf[t0:t0 + CONV_WIN, :] * wb[r + CONV_PAD:r + CONV_PAD + 1, :]
                if r >= 1:
                    z = z + (ubuf[t0 - CONV_RES:t0 - CONV_RES + CONV_WIN, :]
                             * wb[r - CONV_RES + CONV_PAD:r - CONV_RES + CONV_PAD + 1, :])
                zbuf[CONV_WIN * r:CONV_WIN * (r + 1), :] = z
            y = jnp.dot(sel, zbuf[...], preferred_element_type=F32) + b_ref[...]
            mu = jnp.mean(y, axis=-1, keepdims=True)
            dev = y - mu
            var = jnp.mean(dev * dev, axis=-1, keepdims=True)
            zn = dev * lax.rsqrt(var + EPS) * lnw_ref[...] + lnb_ref[...]
            o_ref[r0:r0 + CONV_TILE, :] = _silu(zn).astype(BF16)


N_RET_IN = 10
N_RET_SCRATCH = 6


def _ret_conv_kernel(*refs, need_ctx):
    n_seq = 2 if need_ctx else 1
    n_in = N_RET_IN + 2 * n_seq + 4
    ret_in, conv_in = refs[:N_RET_IN], refs[N_RET_IN:n_in]
    outs, scratch = refs[n_in:n_in + 2 * n_seq], refs[n_in + 2 * n_seq:]
    _ret_kernel(*ret_in, *outs[:n_seq], *scratch[:N_RET_SCRATCH], need_ctx=need_ctx)
    _conv_kernel(*conv_in, *outs[n_seq:], *scratch[N_RET_SCRATCH:], n_seq=n_seq)


def _ret_conv(p_lat, p_ctx, ret_log_decay, gn_w, conv_w, conv_b, ln_w, ln_b, need_ctx):
    w, ch = RET_QK, CONV_CH
    lat = lambda j, width: pl.BlockSpec((None, SEQ, width), lambda b: (b, 0, j))
    ctx = lambda j, width: pl.BlockSpec((None, CTX_LEN, width), lambda b: (b, 0, j))
    vec = lambda width: pl.BlockSpec((1, width), lambda b: (0, 0))
    ret_cols = (OFF_RQ // w, OFF_RK // w, OFF_RV // w, OFF_RG // w)
    seqs = [(lat, SEQ, p_lat)] + ([(ctx, CTX_LEN, p_ctx)] if need_ctx else [])
    in_specs = ([pl.BlockSpec(memory_space=pltpu.SMEM)]
                + [lat(j, w) for j in ret_cols] + [ctx(j, w) for j in ret_cols] + [vec(w)])
    args = [ret_log_decay] + [p_lat] * 4 + [p_ctx] * 4 + [gn_w]
    for spec, _, p in seqs:
        in_specs += [spec(OFF_CV // ch, ch), spec(OFF_CG // ch, ch)]
        args += [p, p]
    in_specs += [pl.BlockSpec((CONV_WIDTH, ch), lambda b: (0, 0)), vec(ch), vec(ch), vec(ch)]
    args += [conv_w, conv_b, ln_w, ln_b]
    outs = pl.pallas_call(
        functools.partial(_ret_conv_kernel, need_ctx=need_ctx),
        grid=(BATCH,),
        in_specs=in_specs,
        out_specs=[spec(0, w) for spec, _, _ in seqs] + [spec(0, ch) for spec, _, _ in seqs],
        out_shape=[jax.ShapeDtypeStruct((BATCH, n, w), BF16) for _, n, _ in seqs]
                  + [jax.ShapeDtypeStruct((BATCH, n, ch), BF16) for _, n, _ in seqs],
        scratch_shapes=[
            pltpu.VMEM((RET_HEADS * RET_C, RET_C), F32),
            pltpu.VMEM((6, RET_C, w), F32),
            pltpu.VMEM((2, RET_NC, w, w), F32),
            pltpu.VMEM((2, RET_NC, w, w), BF16),
            pltpu.VMEM((RET_NC, RET_HEADS * RET_C, RET_C), BF16),
            pltpu.VMEM((RET_NC, RET_C, w), F32),
        ] + [pltpu.VMEM((_conv_rows(n), ch), BF16) for _, n, _ in seqs]
          + [pltpu.VMEM((CONV_RES * CONV_WIN, ch), BF16)],
        compiler_params=_params(1),
        name="retention_conv",
    )(*args)
    if need_ctx:
        return outs[0], outs[1], outs[2], outs[3]
    return outs[0], None, outs[1], None


def _out_mlp_kernel(ret_ref, dif_ref, cnv_ref, x_ref, mod_ref, nw_ref, wo_ref, w1_ref, w2_ref,
                    fin_ref, o_ref, *, final):
    d = D_MODEL
    y = jnp.dot(ret_ref[...], wo_ref[0:RET_WIDTH, :], preferred_element_type=F32)
    y = y + jnp.dot(dif_ref[...], wo_ref[RET_WIDTH:RET_WIDTH + DIFF_WIDTH, :],
                    preferred_element_type=F32)
    y = y + jnp.dot(cnv_ref[...], wo_ref[RET_WIDTH + DIFF_WIDTH:MIX_WIDTH, :],
                    preferred_element_type=F32)
    x1 = x_ref[...] + mod_ref[:, 2 * d:3 * d] * y
    h2 = _rmsnorm_mod(x1, nw_ref[...], mod_ref[:, 3 * d:4 * d], mod_ref[:, 4 * d:5 * d]).astype(BF16)
    u = jnp.maximum(jnp.dot(h2, w1_ref[...], preferred_element_type=F32), 0.0)
    z = jnp.dot((u * u).astype(BF16), w2_ref[...], preferred_element_type=F32)
    x2 = x1 + mod_ref[:, 5 * d:6 * d] * z
    if final:
        ms = jnp.mean(x2 * x2, axis=-1, keepdims=True)
        x2 = x2 * lax.rsqrt(ms + EPS) * fin_ref[...]
    o_ref[...] = x2


def _out_mlp(ret, dif, cnv, xs, mod, nw, wo, w1, w2, layer, fin, tm, is_ctx, final):
    rows = xs.shape[1]
    const = lambda shape: pl.BlockSpec(shape, lambda b, t: (0,) * len(shape),
                                       pipeline_mode=pl.Buffered(1))
    weight = lambda shape: pl.BlockSpec((None,) + shape, lambda b, t: (layer, 0, 0),
                                        pipeline_mode=pl.Buffered(1))
    tile = lambda width: pl.BlockSpec((None, tm, width), lambda b, t: (b, t, 0))
    mod_row = (lambda b, t: (BATCH, 0, 0)) if is_ctx else (lambda b, t: (b, 0, 0))
    return pl.pallas_call(
        functools.partial(_out_mlp_kernel, final=final),
        grid=(BATCH, rows // tm),
        in_specs=[
            tile(RET_WIDTH), tile(DIFF_WIDTH), tile(CONV_CH), tile(D_MODEL),
            pl.BlockSpec((None, 1, N_MOD * D_MODEL), mod_row),
            const((1, D_MODEL)),
            weight((MIX_WIDTH, D_MODEL)), weight((D_MODEL, D_FF)), weight((D_FF, D_MODEL)),
            const((1, D_MODEL)),
        ],
        out_specs=tile(D_MODEL),
        out_shape=jax.ShapeDtypeStruct((BATCH, rows, D_MODEL), F32),
        compiler_params=_params(2),
        name="out_mlp_ctx" if is_ctx else "out_mlp",
    )(ret, dif, cnv, xs, mod, nw, wo, w1, w2, fin)


def kernel(x, c, ctx, c_ctx, norm1_w, norm2_w, ada_w, ada_b, w_in, ret_log_decay, ret_gn_w, diff_lam, diff_gn_w, conv_w, conv_b, conv_ln_w, conv_ln_b, w_out, mlp_w1, mlp_w2, final_norm_w):
    assert x.shape == (BATCH, SEQ, D_MODEL) and ctx.shape == (BATCH, CTX_LEN, D_MODEL)
    tables = _rope_tables()
    cc = jnp.concatenate([c, c_ctx[None, :], jnp.zeros((MOD_ROWS - BATCH - 1, D_MODEL), F32)], axis=0)
    mod_all = _ada(cc, ada_w, ada_b).reshape(DEPTH, MOD_ROWS, 1, N_MOD * D_MODEL)
    fin = final_norm_w.reshape(1, D_MODEL)
    w_bf = _reorder_in_proj(w_in).astype(BF16)
    wo, w1, w2 = w_out.astype(BF16), mlp_w1.astype(BF16), mlp_w2.astype(BF16)
    for i in range(DEPTH):
        last = i == DEPTH - 1
        lam_init = 0.8 - 0.6 * math.exp(-0.3 * i)
        mod = mod_all[i]
        nw1 = norm1_w[i].reshape(1, D_MODEL)
        nw2 = norm2_w[i].reshape(1, D_MODEL)
        dgn = diff_gn_w[i].reshape(1, DIFF_WIDTH)
        p_lat, vt_lat = _inproj(x, mod, nw1, w_bf, i, tables, TM_LAT)
        p_ctx, vt_ctx = _inproj(ctx, mod, nw1, w_bf, i, None, TM_CTX)
        ret_lat, ret_ctx, cnv_lat, cnv_ctx = _ret_conv(
            p_lat, p_ctx, ret_log_decay[i], ret_gn_w[i].reshape(1, RET_WIDTH),
            conv_w[i], conv_b[i].reshape(1, CONV_CH), conv_ln_w[i].reshape(1, CONV_CH),
            conv_ln_b[i].reshape(1, CONV_CH), not last)
        dif_lat = _diff_attention(p_lat, [p_lat, p_ctx], [vt_lat, vt_ctx], diff_lam[i], dgn, lam_init,
                                  DIFF_QT)
        if not last:
            dif_ctx = _diff_attention(p_ctx, [p_ctx], [vt_ctx], diff_lam[i], dgn, lam_init, CTX_LEN)
            ctx = _out_mlp(ret_ctx, dif_ctx, cnv_ctx, ctx, mod, nw2, wo, w1, w2, i, fin, TM_CTX, True, False)
        x = _out_mlp(ret_lat, dif_lat, cnv_lat, x, mod, nw2, wo, w1, w2, i, fin, TM_LAT, False, last)
    return x
```
